```python
import jax, jax.numpy as jnp
from jax import lax
import numpy as np

D_MODEL = 1024
BATCH = 8
SEQ = 4096
DEPTH = 4

N_META = 16
CHUNK = 64
NORM_EPS = 1e-6
GLA_HEADS = 4
GLA_DK = 128
GLA_DV = 256
GLA_KEY = GLA_HEADS * GLA_DK
GLA_VAL = GLA_HEADS * GLA_DV
GLA_GATE_RANK = 16
GLA_GATE_NORM = 16.0
GLA_LOG_DECAY_MIN = -1.0
DN_HEADS = 8
DN_DK = 128
DN_DV = 128
DN_KEY = DN_HEADS * DN_DK
DN_VAL = DN_HEADS * DN_DV
DN_CONV = 4
D_FF = 4 * D_MODEL
IN_SIZES = (GLA_KEY, GLA_KEY, GLA_VAL, GLA_GATE_RANK, GLA_VAL,
            2 * DN_KEY + DN_VAL, DN_VAL, DN_HEADS, DN_HEADS,
            D_MODEL, D_MODEL)
IN_COLS = sum(IN_SIZES)

kernel_name = "hybrid_gla_gdn_meta_block"


def rms_norm(x, w):
    xf = x.astype(jnp.float32)
    y = xf * lax.rsqrt(jnp.mean(xf * xf, axis=-1, keepdims=True) + NORM_EPS)
    return (y * w.astype(jnp.float32)).astype(x.dtype)


def l2_normalize(x):
    xf = x.astype(jnp.float32)
    return xf * lax.rsqrt(jnp.sum(xf * xf, axis=-1, keepdims=True) + NORM_EPS)


def causal_depthwise_conv(x, w):
    k_width, channels = w.shape
    return lax.conv_general_dilated(
        x, w[:, None, :].astype(x.dtype), window_strides=(1,), padding=[(k_width - 1, 0)],
        dimension_numbers=('NWC', 'WIO', 'NWC'), feature_group_count=channels)


def to_chunks(x, pad):
    x = jnp.pad(x, ((0, 0), (pad, 0), (0, 0), (0, 0)))
    b_, lp, h, d = x.shape
    return x.reshape(b_, lp // CHUNK, CHUNK, h, d).transpose(0, 3, 1, 2, 4)


def from_chunks(o, pad):
    b_, h, n, c, d = o.shape
    return o.transpose(0, 2, 3, 1, 4).reshape(b_, n * c, h, d)[:, pad:]


def gla_chunked(q, k, v, log_a):
    b_, length, h, dk = q.shape
    dv = v.shape[-1]
    pad = (-length) % CHUNK
    qc, kc, vc, gc = (to_chunks(t.astype(jnp.float32), pad) for t in (q, k, v, log_a))
    cum = jnp.cumsum(gc, axis=3)
    cum_last = cum[:, :, :, -1:, :]
    qe = qc * jnp.exp(cum) * (dk ** -0.5)
    ke = kc * jnp.exp(-cum)
    kd = kc * jnp.exp(cum_last - cum)
    idx = jnp.arange(CHUNK)
    incl = idx[:, None] >= idx[None, :]
    scores = jnp.where(incl, jnp.einsum('bhncd,bhnsd->bhncs', qe, ke), 0.0)
    o_intra = jnp.einsum('bhncs,bhnsv->bhncv', scores, vc)

    def step(state, xs):
        qe_n, kd_n, v_n, dec_n = xs
        o_n = jnp.einsum('bhcd,bhdv->bhcv', qe_n, state)
        state = state * dec_n[..., None] + jnp.einsum('bhcd,bhcv->bhdv', kd_n, v_n)
        return state, o_n

    s0 = jnp.zeros((b_, h, dk, dv), jnp.float32)
    xs = tuple(jnp.moveaxis(t, 2, 0) for t in (qe, kd, vc, jnp.exp(cum_last[:, :, :, 0, :])))
    _, o_inter = lax.scan(step, s0, xs)
    return from_chunks(o_intra + jnp.moveaxis(o_inter, 0, 2), pad)


def gated_delta_chunked(q, k, v, beta, log_a):
    b_, length, h, dk = q.shape
    dv = v.shape[-1]
    pad = (-length) % CHUNK
    qc, kc, vc = (to_chunks(t.astype(jnp.float32), pad) for t in (q, k, v))
    bc = to_chunks(beta.astype(jnp.float32)[..., None], pad)[..., 0]
    gc = to_chunks(log_a.astype(jnp.float32)[..., None], pad)[..., 0]
    gam = jnp.cumsum(gc, axis=-1)
    idx = jnp.arange(CHUNK)
    incl = idx[:, None] >= idx[None, :]
    strict = idx[:, None] > idx[None, :]
    decay = jnp.exp(jnp.where(incl, gam[..., :, None] - gam[..., None, :], -jnp.inf))
    kb = kc * bc[..., None]
    a_kk = jnp.where(strict, jnp.einsum('bhncd,bhnsd->bhncs', kb, kc) * decay, 0.0)
    tri = a_kk + jnp.eye(CHUNK, dtype=jnp.float32)
    u = lax.linalg.triangular_solve(tri, vc * bc[..., None], left_side=True, lower=True, unit_diagonal=True)
    w = lax.linalg.triangular_solve(tri, kb * jnp.exp(gam)[..., None], left_side=True, lower=True, unit_diagonal=True)
    qs = qc * (dk ** -0.5)
    a_qk = jnp.einsum('bhncd,bhnsd->bhncs', qs, kc) * decay
    qd = qs * jnp.exp(gam)[..., None]
    kd = kc * jnp.exp(gam[..., -1:] - gam)[..., None]

    def step(state, xs):
        qd_n, kd_n, w_n, u_n, aqk_n, dec_n = xs
        v_new = u_n - jnp.einsum('bhcd,bhdv->bhcv', w_n, state)
        o_n = jnp.einsum('bhcd,bhdv->bhcv', qd_n, state) + jnp.einsum('bhcs,bhsv->bhcv', aqk_n, v_new)
        state = state * dec_n[..., None, None] + jnp.einsum('bhcd,bhcv->bhdv', kd_n, v_new)
        return state, o_n

    s0 = jnp.zeros((b_, h, dk, dv), jnp.float32)
    xs = tuple(jnp.moveaxis(t, 2, 0) for t in (qd, kd, w, u, a_qk, jnp.exp(gam[..., -1])))
    _, o = lax.scan(step, s0, xs)
    return from_chunks(jnp.moveaxis(o, 0, 2), pad)


def hybrid_mixer(u, w_in, gla_w_gate_up, gla_b_gate, gla_norm_w, dn_conv_w, dn_a_log, dn_dt_bias,
                 dn_norm_w, w_branch_gla, w_branch_dn, w_out):
    b_, length, _ = u.shape
    proj = u @ w_in
    split_points = [int(s) for s in np.cumsum(IN_SIZES)[:-1]]
    (g_q, g_k, g_v, g_lr, g_z, d_qkv, d_z, d_b, d_a, gate_gla, gate_dn) = jnp.split(proj, split_points, axis=-1)

    q = g_q.reshape(b_, length, GLA_HEADS, GLA_DK)
    k = g_k.reshape(b_, length, GLA_HEADS, GLA_DK)
    v = g_v.reshape(b_, length, GLA_HEADS, GLA_DV)
    gate_logit = (g_lr @ gla_w_gate_up + gla_b_gate).astype(jnp.float32)
    log_a = jnp.maximum(jax.nn.log_sigmoid(gate_logit) / GLA_GATE_NORM, GLA_LOG_DECAY_MIN)
    o = gla_chunked(q, k, v, log_a.reshape(b_, length, GLA_HEADS, GLA_DK)).astype(u.dtype)
    o = rms_norm(o, gla_norm_w) * jax.nn.silu(g_z.reshape(b_, length, GLA_HEADS, GLA_DV))
    o_gla = o.reshape(b_, length, GLA_VAL)

    qkv = jax.nn.silu(causal_depthwise_conv(d_qkv, dn_conv_w))
    dq, dk_, dv_ = jnp.split(qkv, [DN_KEY, 2 * DN_KEY], axis=-1)
    dq = l2_normalize(dq.reshape(b_, length, DN_HEADS, DN_DK))
    dk_ = l2_normalize(dk_.reshape(b_, length, DN_HEADS, DN_DK))
    dv_ = dv_.reshape(b_, length, DN_HEADS, DN_DV)
    beta = jax.nn.sigmoid(d_b.astype(jnp.float32))
    log_alpha = -jnp.exp(dn_a_log.astype(jnp.float32)) * jax.nn.softplus(
        d_a.astype(jnp.float32) + dn_dt_bias.astype(jnp.float32))
    o = gated_delta_chunked(dq, dk_, dv_, beta, log_alpha).astype(u.dtype)
    o = rms_norm(o, dn_norm_w) * jax.nn.silu(d_z.reshape(b_, length, DN_HEADS, DN_DV))
    o_dn = o.reshape(b_, length, DN_VAL)

    merged = jax.nn.sigmoid(gate_gla) * (o_gla @ w_branch_gla) + jax.nn.sigmoid(gate_dn) * (o_dn @ w_branch_dn)
    return merged @ w_out


def squared_relu_mlp(u, w_up, w_down):
    return jnp.square(jax.nn.relu(u @ w_up)) @ w_down


def setup_inputs(seed: int = 0) -> dict:
    key = jax.random.key(seed)
    ks = jax.random.split(key, 20)
    f32 = jnp.float32

    def nrm(k, shape, scale):
        return jax.random.normal(k, shape, f32) * scale

    def gain(k, shape):
        return 1.0 + 0.02 * jax.random.normal(k, shape, f32)

    dt = jnp.exp(jax.random.uniform(ks[9], (DEPTH, DN_HEADS), f32) * (np.log(0.1) - np.log(1e-3)) + np.log(1e-3))
    return {
        'x': nrm(ks[0], (BATCH, SEQ, D_MODEL), 1.0),
        'meta_tokens': nrm(ks[1], (N_META, D_MODEL), 1.0),
        'mixer_norm_w': gain(ks[2], (DEPTH, D_MODEL)),
        'w_in': nrm(ks[3], (DEPTH, D_MODEL, IN_COLS), D_MODEL ** -0.5),
        'gla_w_gate_up': nrm(ks[4], (DEPTH, GLA_GATE_RANK, GLA_KEY), GLA_GATE_RANK ** -0.5),
        'gla_b_gate': nrm(ks[5], (DEPTH, GLA_KEY), 0.1),
        'gla_norm_w': gain(ks[6], (DEPTH, GLA_DV)),
        'dn_conv_w': nrm(ks[7], (DEPTH, DN_CONV, 2 * DN_KEY + DN_VAL), DN_CONV ** -0.5),
        'dn_a_log': jnp.log(jax.random.uniform(ks[8], (DEPTH, DN_HEADS), f32, 1.0, 16.0)),
        'dn_dt_bias': dt + jnp.log(-jnp.expm1(-dt)),
        'dn_norm_w': gain(ks[10], (DEPTH, DN_DV)),
        'w_branch_gla': nrm(ks[11], (DEPTH, GLA_VAL, D_MODEL), GLA_VAL ** -0.5),
        'w_branch_dn': nrm(ks[12], (DEPTH, DN_VAL, D_MODEL), DN_VAL ** -0.5),
        'w_out': nrm(ks[13], (DEPTH, D_MODEL, D_MODEL), D_MODEL ** -0.5),
        'mlp_norm_w': gain(ks[14], (DEPTH, D_MODEL)),
        'w_mlp_up': nrm(ks[15], (DEPTH, D_MODEL, D_FF), D_MODEL ** -0.5),
        'w_mlp_down': nrm(ks[16], (DEPTH, D_FF, D_MODEL), D_FF ** -0.5),
        'final_norm_w': gain(ks[17], (D_MODEL,)),
    }


def reference(x, meta_tokens, mixer_norm_w, w_in, gla_w_gate_up, gla_b_gate, gla_norm_w, dn_conv_w,
              dn_a_log, dn_dt_bias, dn_norm_w, w_branch_gla, w_branch_dn, w_out, mlp_norm_w,
              w_mlp_up, w_mlp_down, final_norm_w):
    b_ = x.shape[0]
    meta = jnp.broadcast_to(meta_tokens[None].astype(x.dtype), (b_, N_META, x.shape[-1]))
    h = jnp.concatenate([meta, x], axis=1)
    for layer in range(DEPTH):
        h = h + hybrid_mixer(rms_norm(h, mixer_norm_w[layer]), w_in[layer], gla_w_gate_up[layer],
                             gla_b_gate[layer], gla_norm_w[layer], dn_conv_w[layer], dn_a_log[layer],
                             dn_dt_bias[layer], dn_norm_w[layer], w_branch_gla[layer], w_branch_dn[layer],
                             w_out[layer])
        h = h + squared_relu_mlp(rms_norm(h, mlp_norm_w[layer]), w_mlp_up[layer], w_mlp_down[layer])
    return rms_norm(h[:, N_META:], final_norm_w)
```

```python
import functools

import jax
import jax.numpy as jnp
from jax import lax
from jax.experimental import pallas as pl
from jax.experimental.pallas import tpu as pltpu

D_MODEL = 1024
N_META = 16
CHUNK = 64
NORM_EPS = 1e-6
GLA_HEADS = 4
GLA_DK = 128
GLA_DV = 256
GLA_KEY = GLA_HEADS * GLA_DK
GLA_VAL = GLA_HEADS * GLA_DV
GLA_GATE_RANK = 16
GLA_GATE_NORM = 16.0
GLA_LOG_DECAY_MIN = -1.0
DN_HEADS = 8
DN_DK = 128
DN_DV = 128
DN_KEY = DN_HEADS * DN_DK
DN_VAL = DN_HEADS * DN_DV
DN_CONV = 4
D_FF = 4 * D_MODEL

MAIN_COLS = 2 * GLA_KEY + 2 * GLA_VAL + (2 * DN_KEY + DN_VAL) + DN_VAL + 2 * D_MODEL
LANES = 128
SMALL_COLS = LANES
LR_LANE0 = 0
DB_LANE0 = GLA_GATE_RANK
DA_LANE0 = GLA_GATE_RANK + DN_HEADS

VMEM_LIMIT = 56 * 1024 * 1024
BF16 = jnp.bfloat16
F32 = jnp.float32

_NT = (((1,), (1,)), ((), ()))
_TN = (((0,), (0,)), ((), ()))


def _dot(a, b):
    return jnp.dot(a, b, preferred_element_type=F32)


def _dot_nt(a, b):
    return lax.dot_general(a, b, _NT, preferred_element_type=F32)


def _dot_tn(a, b):
    return lax.dot_general(a, b, _TN, preferred_element_type=F32)


def _split_bf16(x, parts):
    out = []
    r = x
    for _ in range(parts):
        p = r.astype(BF16)
        out.append(p)
        r = r - p.astype(F32)
    return out


def _dot_exact_left(m_bf16, x, parts=3):
    acc = None
    for p in _split_bf16(x, parts):
        t = _dot(m_bf16, p)
        acc = t if acc is None else acc + t
    return acc


def _dot_exact_right(x, m_bf16, parts=3):
    acc = None
    for p in _split_bf16(x, parts):
        t = _dot(p, m_bf16)
        acc = t if acc is None else acc + t
    return acc


def _softplus(x):
    return jnp.maximum(x, 0.0) + jnp.log1p(jnp.exp(-jnp.abs(x)))


def _silu(x):
    return x * jax.nn.sigmoid(x)


def _row_tile(total_rows, target):
    n = total_rows // CHUNK
    best = 1
    for d in range(1, n + 1):
        if n % d == 0 and d * CHUNK <= target:
            best = d
    return best * CHUNK


def _col_tile(total_cols, target):
    n = total_cols // LANES
    best = 1
    for d in range(1, n + 1):
        if n % d == 0 and d * LANES <= target:
            best = d
    return best * LANES


def _const_spec(shape):
    nd = len(shape)
    return pl.BlockSpec(shape, lambda *_: (0,) * nd, pipeline_mode=pl.Buffered(1))


def _inproj_kernel(h_ref, nw_ref, w_ref, ws_ref, o_ref, os_ref, xn_ref):
    @pl.when(pl.program_id(1) == 0)
    def _():
        x = h_ref[...]
        ms = jnp.mean(x * x, axis=-1, keepdims=True)
        xn = ((x * lax.rsqrt(ms + NORM_EPS)) * nw_ref[...]).astype(BF16)
        xn_ref[...] = xn
        os_ref[...] = _dot(xn, ws_ref[...])

    o_ref[...] = _dot(xn_ref[...], w_ref[...]).astype(BF16)


def _inproj(h, norm_w, w_main, w_small):
    rows, d = h.shape
    tm = _row_tile(rows, 1280)
    tn = _col_tile(MAIN_COLS, 1536)
    return pl.pallas_call(
        _inproj_kernel,
        out_shape=(jax.ShapeDtypeStruct((rows, MAIN_COLS), BF16),
                   jax.ShapeDtypeStruct((rows, SMALL_COLS), F32)),
        grid=(rows // tm, MAIN_COLS // tn),
        in_specs=[
            pl.BlockSpec((tm, d), lambda i, j: (i, 0)),
            _const_spec((1, d)),
            pl.BlockSpec((d, tn), lambda i, j: (0, j)),
            _const_spec((d, SMALL_COLS)),
        ],
        out_specs=(pl.BlockSpec((tm, tn), lambda i, j: (i, j)),
                   pl.BlockSpec((tm, SMALL_COLS), lambda i, j: (i, 0))),
        scratch_shapes=[pltpu.VMEM((tm, d), BF16)],
        compiler_params=pltpu.CompilerParams(
            dimension_semantics=("arbitrary", "arbitrary"), vmem_limit_bytes=VMEM_LIMIT),
        name="inproj",
    )(h, norm_w, w_main, w_small)


def _gla_kernel(q_ref, k_ref, v_ref, z_ref, sm_ref, wg_ref, bg_ref, nw_ref, o_ref, st_ref, *, pad):
    n = pl.program_id(1)
    c = CHUNK

    @pl.when(n == 0)
    def _():
        st_ref[...] = jnp.zeros_like(st_ref)

    row = lax.broadcasted_iota(jnp.int32, (c, 1), 0)
    valid = jnp.logical_or(n > 0, row >= pad)
    ri = lax.broadcasted_iota(jnp.int32, (c, c), 0)
    ci = lax.broadcasted_iota(jnp.int32, (c, c), 1)
    incl = ri >= ci
    tri = jnp.where(incl, 1.0, 0.0).astype(BF16)

    logit = _dot(sm_ref[...].astype(BF16), wg_ref[...]) + bg_ref[...]
    log_sig = jnp.minimum(logit, 0.0) - jnp.log1p(jnp.exp(-jnp.abs(logit)))
    g = jnp.maximum(log_sig / GLA_GATE_NORM, GLA_LOG_DECAY_MIN)
    g = jnp.where(valid, g, 0.0)
    cum = _dot_exact_left(tri, g)
    cl = cum[c - 1:c, :]
    q = q_ref[...].astype(F32)
    k = k_ref[...].astype(F32)
    qe = ((q * jnp.exp(cum)) * (GLA_DK ** -0.5)).astype(BF16)
    ke = (k * jnp.exp(-cum)).astype(BF16)
    kd = (k * jnp.exp(cl - cum)).astype(BF16)
    dec = jnp.exp(cl)
    nw = nw_ref[...]

    for h in range(GLA_HEADS):
        ks = slice(h * GLA_DK, (h + 1) * GLA_DK)
        vs = slice(h * GLA_DV, (h + 1) * GLA_DV)
        qe_h = qe[:, ks]
        v_h = v_ref[:, vs]
        scores = jnp.where(incl, _dot_nt(qe_h, ke[:, ks]), 0.0)
        st = st_ref[h]
        o = _dot(scores.astype(BF16), v_h) + _dot_nt(qe_h, st.astype(BF16))
        st_ref[h] = st * dec[:, ks] + _dot_tn(v_h, kd[:, ks])
        ms = jnp.mean(o * o, axis=-1, keepdims=True)
        y = (o * lax.rsqrt(ms + NORM_EPS)) * nw
        o_ref[:, vs] = (y * _silu(z_ref[:, vs].astype(F32))).astype(BF16)


def _gla(proj, small, wg, bg, nw, batch, nc, pad):
    c = CHUNK
    rb = lambda b, n: b * nc + n
    kern = functools.partial(_gla_kernel, pad=pad)
    return pl.pallas_call(
        kern,
        out_shape=jax.ShapeDtypeStruct((batch * nc * c, GLA_VAL), BF16),
        grid=(batch, nc),
        in_specs=[
            pl.BlockSpec((c, GLA_KEY), lambda b, n: (rb(b, n), 0)),
            pl.BlockSpec((c, GLA_KEY), lambda b, n: (rb(b, n), 1)),
            pl.BlockSpec((c, GLA_VAL), lambda b, n: (rb(b, n), 1)),
            pl.BlockSpec((c, GLA_VAL), lambda b, n: (rb(b, n), 2)),
            pl.BlockSpec((c, SMALL_COLS), lambda b, n: (rb(b, n), 0)),
            _const_spec((SMALL_COLS, GLA_KEY)),
            _const_spec((1, GLA_KEY)),
            _const_spec((1, GLA_DV)),
        ],
        out_specs=pl.BlockSpec((c, GLA_VAL), lambda b, n: (rb(b, n), 0)),
        scratch_shapes=[pltpu.VMEM((GLA_HEADS, GLA_DV, GLA_DK), F32)],
        compiler_params=pltpu.CompilerParams(
            dimension_semantics=("arbitrary", "arbitrary"), vmem_limit_bytes=VMEM_LIMIT),
        name="gla",
    )(proj, proj, proj, proj, small, wg, bg, nw)


def _unit_lower_inverse(a, ri, ci):
    eye = jnp.where(ri == ci, 1.0, 0.0)
    base = 8
    blk = lambda idx, size: lax.shift_right_logical(idx, size.bit_length() - 1)
    dm = jnp.where(blk(ri, base) == blk(ci, base), a, 0.0)
    d = dm.astype(BF16)
    d2 = _dot(d, d)
    t = eye - dm
    t = t + _dot(t.astype(BF16), d2.astype(BF16))
    d2b = d2.astype(BF16)
    d4 = _dot(d2b, d2b)
    t = t + _dot(t.astype(BF16), d4.astype(BF16))
    b = base
    while b < CHUNK:
        lmask = jnp.logical_and(blk(ri, 2 * b) == blk(ci, 2 * b), blk(ri, b) != blk(ci, b))
        l = jnp.where(lmask, a, 0.0).astype(BF16)
        tb = t.astype(BF16)
        t = t - _dot(_dot(tb, l).astype(BF16), tb)
        b *= 2
    return t


def _dn_kernel(qkv_ref, z_ref, sm_ref, cw_ref, alog_ref, dtb_ref, nw_ref, eb_ref, ea_ref, ea64_ref,
               o_ref, xbuf, st_ref, *, pad):
    n = pl.program_id(1)
    c = CHUNK
    tail = 8

    @pl.when(n == 0)
    def _():
        xbuf[0:tail, :] = jnp.zeros((tail, xbuf.shape[1]), F32)
        st_ref[...] = jnp.zeros_like(st_ref)

    row = lax.broadcasted_iota(jnp.int32, (c, 1), 0)
    valid = jnp.logical_or(n > 0, row >= pad)
    ri = lax.broadcasted_iota(jnp.int32, (c, c), 0)
    ci = lax.broadcasted_iota(jnp.int32, (c, c), 1)
    incl = ri >= ci
    strict = ri > ci
    tri = jnp.where(incl, 1.0, 0.0).astype(BF16)

    sm = sm_ref[...]
    beta_blk = jnp.where(valid, jax.nn.sigmoid(sm), 0.0)
    la_blk = jnp.where(valid, -jnp.exp(alog_ref[...]) * _softplus(sm + dtb_ref[...]), 0.0)
    beta128 = _dot_exact_right(beta_blk, eb_ref[...], parts=2)
    gam_blk = _dot_exact_left(tri, la_blk)
    gam128 = _dot_exact_right(gam_blk, ea_ref[...])
    la64 = _dot_exact_right(la_blk, ea64_ref[...])
    sj = lax.broadcasted_iota(jnp.int32, (c, DN_HEADS * c), 0)
    ss = jnp.bitwise_and(lax.broadcasted_iota(jnp.int32, (c, DN_HEADS * c), 1), c - 1)
    dsum = _dot_exact_left(tri, jnp.where(sj > ss, la64, 0.0))
    decay_all = jnp.where(sj >= ss, jnp.exp(dsum), 0.0)

    xbuf[tail:tail + c, :] = qkv_ref[...].astype(F32)
    nw = nw_ref[...]
    scale = DN_DK ** -0.5

    def conv_silu(col0):
        cs = slice(col0, col0 + LANES)
        y = cw_ref[DN_CONV - 1:DN_CONV, cs] * xbuf[tail:tail + c, cs]
        for j in range(DN_CONV - 1):
            off = tail - (DN_CONV - 1) + j
            y = y + cw_ref[j:j + 1, cs] * xbuf[off:off + c, cs]
        return _silu(y)

    def l2n(x):
        return x * lax.rsqrt(jnp.sum(x * x, axis=-1, keepdims=True) + NORM_EPS)

    for h in range(DN_HEADS):
        hs = slice(h * DN_DK, (h + 1) * DN_DK)
        q_h = l2n(conv_silu(h * DN_DK))
        k_h = l2n(conv_silu(DN_KEY + h * DN_DK))
        v_h = conv_silu(2 * DN_KEY + h * DN_DV)
        b_h = beta128[:, hs]
        g_h = gam128[:, hs]
        g_last = g_h[c - 1:c, :]
        eg = jnp.exp(g_h)
        dm = decay_all[:, h * c:(h + 1) * c]
        kb = k_h * b_h
        k_bf = k_h.astype(BF16)
        a_kk = jnp.where(strict, _dot_nt(kb.astype(BF16), k_bf) * dm, 0.0)
        t_bf = _unit_lower_inverse(a_kk, ri, ci).astype(BF16)
        u = _dot(t_bf, (v_h * b_h).astype(BF16))
        w = _dot(t_bf, (kb * eg).astype(BF16))
        qs = q_h * scale
        a_qk = _dot_nt(qs.astype(BF16), k_bf) * dm
        qd = (qs * eg).astype(BF16)
        kd = (k_h * jnp.exp(g_last - g_h)).astype(BF16)
        st = st_ref[h]
        st_bf = st.astype(BF16)
        v_new = u - _dot(w.astype(BF16), st_bf)
        v_new_bf = v_new.astype(BF16)
        o = _dot(qd, st_bf) + _dot(a_qk.astype(BF16), v_new_bf)
        st_ref[h] = st * jnp.exp(g_last) + _dot_tn(kd, v_new_bf)
        ms = jnp.mean(o * o, axis=-1, keepdims=True)
        y = (o * lax.rsqrt(ms + NORM_EPS)) * nw
        o_ref[:, hs] = (y * _silu(z_ref[:, hs].astype(F32))).astype(BF16)

    xbuf[0:tail, :] = xbuf[c:c + tail, :]


def _dn(proj, small, cw, alog, dtb, nw, eb, ea, ea64, batch, nc, pad):
    c = CHUNK
    qkv_cols = 2 * DN_KEY + DN_VAL
    rb = lambda b, n: b * nc + n
    kern = functools.partial(_dn_kernel, pad=pad)
    return pl.pallas_call(
        kern,
        out_shape=jax.ShapeDtypeStruct((batch * nc * c, DN_VAL), BF16),
        grid=(batch, nc),
        in_specs=[
            pl.BlockSpec((c, qkv_cols), lambda b, n: (rb(b, n), 1)),
            pl.BlockSpec((c, DN_VAL), lambda b, n: (rb(b, n), 6)),
            pl.BlockSpec((c, SMALL_COLS), lambda b, n: (rb(b, n), 0)),
            _const_spec((DN_CONV, qkv_cols)),
            _const_spec((1, SMALL_COLS)),
            _const_spec((1, SMALL_COLS)),
            _const_spec((1, DN_DV)),
            _const_spec((SMALL_COLS, DN_HEADS * LANES)),
            _const_spec((SMALL_COLS, DN_HEADS * LANES)),
            _const_spec((SMALL_COLS, DN_HEADS * c)),
        ],
        out_specs=pl.BlockSpec((c, DN_VAL), lambda b, n: (rb(b, n), 0)),
        scratch_shapes=[pltpu.VMEM((c + 8, qkv_cols), F32),
                        pltpu.VMEM((DN_HEADS, DN_DK, DN_DV), F32)],
        compiler_params=pltpu.CompilerParams(
            dimension_semantics=("arbitrary", "arbitrary"), vmem_limit_bytes=VMEM_LIMIT),
        name="deltanet",
    )(proj, proj, small, cw, alog, dtb, nw, eb, ea, ea64)


def _merge_kernel(og_ref, od_ref, gg_ref, gd_ref, h_ref, wbg_ref, wbd_ref, wo_ref, o_ref):
    pg = _dot(og_ref[...], wbg_ref[...])
    pd = _dot(od_ref[...], wbd_ref[...])
    merged = (jax.nn.sigmoid(gg_ref[...].astype(F32)) * pg
              + jax.nn.sigmoid(gd_ref[...].astype(F32)) * pd)
    o_ref[...] = h_ref[...] + _dot(merged.astype(BF16), wo_ref[...])


def _merge(o_gla, o_dn, proj, h, wbg, wbd, wo):
    rows, d = h.shape
    tm = _row_tile(rows, 640)
    row_spec = lambda cols, j: pl.BlockSpec((tm, cols), lambda i: (i, j))
    return pl.pallas_call(
        _merge_kernel,
        out_shape=jax.ShapeDtypeStruct((rows, d), F32),
        grid=(rows // tm,),
        in_specs=[
            row_spec(GLA_VAL, 0),
            row_spec(DN_VAL, 0),
            row_spec(d, 7),
            row_spec(d, 8),
            row_spec(d, 0),
            _const_spec((GLA_VAL, d)),
            _const_spec((DN_VAL, d)),
            _const_spec((d, d)),
        ],
        out_specs=row_spec(d, 0),
        compiler_params=pltpu.CompilerParams(
            dimension_semantics=("arbitrary",), vmem_limit_bytes=VMEM_LIMIT),
        name="merge",
    )(o_gla, o_dn, proj, proj, h, wbg, wbd, wo)


def _mlp_kernel(h_ref, nw_ref, wu_ref, wd_ref, o_ref, *, ff_tile):
    x = h_ref[...]
    ms = jnp.mean(x * x, axis=-1, keepdims=True)
    xn = ((x * lax.rsqrt(ms + NORM_EPS)) * nw_ref[...]).astype(BF16)
    acc = x
    for f in range(0, D_FF, ff_tile):
        up = jnp.maximum(_dot(xn, wu_ref[:, f:f + ff_tile]), 0.0)
        acc = acc + _dot((up * up).astype(BF16), wd_ref[f:f + ff_tile, :])
    o_ref[...] = acc


def _mlp(h, norm_w, wu, wd):
    rows, d = h.shape
    tm = _row_tile(rows, 640)
    kern = functools.partial(_mlp_kernel, ff_tile=1024)
    return pl.pallas_call(
        kern,
        out_shape=jax.ShapeDtypeStruct((rows, d), F32),
        grid=(rows // tm,),
        in_specs=[
            pl.BlockSpec((tm, d), lambda i: (i, 0)),
            _const_spec((1, d)),
            _const_spec((d, D_FF)),
            _const_spec((D_FF, d)),
        ],
        out_specs=pl.BlockSpec((tm, d), lambda i: (i, 0)),
        compiler_params=pltpu.CompilerParams(
            dimension_semantics=("arbitrary",), vmem_limit_bytes=VMEM_LIMIT),
        name="mlp",
    )(h, norm_w, wu, wd)


def _norm_kernel(h_ref, nw_ref, o_ref):
    x = h_ref[...]
    ms = jnp.mean(x * x, axis=-1, keepdims=True)
    o_ref[...] = (x * lax.rsqrt(ms + NORM_EPS)) * nw_ref[...]


def _final_norm(h, norm_w):
    rows, d = h.shape
    tm = _row_tile(rows, 1280)
    return pl.pallas_call(
        _norm_kernel,
        out_shape=jax.ShapeDtypeStruct((rows, d), F32),
        grid=(rows // tm,),
        in_specs=[pl.BlockSpec((tm, d), lambda i: (i, 0)), _const_spec((1, d))],
        out_specs=pl.BlockSpec((tm, d), lambda i: (i, 0)),
        compiler_params=pltpu.CompilerParams(
            dimension_semantics=("arbitrary",), vmem_limit_bytes=VMEM_LIMIT),
        name="final_norm",
    )(h, norm_w)


def _lane_expand_matrix(lane0, width):
    r = jnp.arange(SMALL_COLS)[:, None]
    col_head = jnp.arange(DN_HEADS * width)[None, :] // width
    return (r == lane0 + col_head).astype(BF16)


def _pad_lanes(v, lane0):
    return jnp.zeros((1, SMALL_COLS), F32).at[0, lane0:lane0 + v.shape[0]].set(v.astype(F32))


def kernel(x, meta_tokens, mixer_norm_w, w_in, gla_w_gate_up, gla_b_gate, gla_norm_w, dn_conv_w,
           dn_a_log, dn_dt_bias, dn_norm_w, w_branch_gla, w_branch_dn, w_out, mlp_norm_w,
           w_mlp_up, w_mlp_down, final_norm_w):
    batch, seq, d = x.shape
    depth = w_in.shape[0]
    pad = (-(N_META + seq)) % CHUNK
    lp = pad + N_META + seq
    nc = lp // CHUNK
    meta = jnp.broadcast_to(meta_tokens[None].astype(x.dtype), (batch, N_META, d))
    h = jnp.concatenate([jnp.zeros((batch, pad, d), x.dtype), meta, x], axis=1).reshape(batch * lp, d)

    o_lr = 2 * GLA_KEY + GLA_VAL
    o_z = o_lr + GLA_GATE_RANK
    o_db = o_z + GLA_VAL + (2 * DN_KEY + DN_VAL) + DN_VAL
    o_gate = o_db + 2 * DN_HEADS

    eb = _lane_expand_matrix(DB_LANE0, LANES)
    ea = _lane_expand_matrix(DA_LANE0, LANES)
    ea64 = _lane_expand_matrix(DA_LANE0, CHUNK)

    for l in range(depth):
        w = w_in[l]
        w_main = jnp.concatenate([w[:, :o_lr], w[:, o_z:o_db], w[:, o_gate:]], axis=1).astype(BF16)
        w_small = jnp.concatenate(
            [w[:, o_lr:o_z], w[:, o_db:o_gate],
             jnp.zeros((d, SMALL_COLS - GLA_GATE_RANK - 2 * DN_HEADS), w.dtype)], axis=1).astype(BF16)
        wg = jnp.zeros((SMALL_COLS, GLA_KEY), F32).at[LR_LANE0:LR_LANE0 + GLA_GATE_RANK].set(
            gla_w_gate_up[l]).astype(BF16)

        proj, small = _inproj(h, mixer_norm_w[l][None, :], w_main, w_small)
        o_gla = _gla(proj, small, wg, gla_b_gate[l][None, :], gla_norm_w[l][None, :], batch, nc, pad)
        o_dn = _dn(proj, small, dn_conv_w[l], _pad_lanes(dn_a_log[l], DA_LANE0),
                   _pad_lanes(dn_dt_bias[l], DA_LANE0), dn_norm_w[l][None, :], eb, ea, ea64,
                   batch, nc, pad)
        h = _merge(o_gla, o_dn, proj, h, w_branch_gla[l].astype(BF16), w_branch_dn[l].astype(BF16),
                   w_out[l].astype(BF16))
        h = _mlp(h, mlp_norm_w[l][None, :], w_mlp_up[l].astype(BF16), w_mlp_down[l].astype(BF16))

    out = _final_norm(h, final_norm_w[None, :])
    return out.reshape(batch, lp, d)[:, pad + N_META:]
```

```python
import functools

import jax
import jax.numpy as jnp
from jax import lax
from jax.experimental import pallas as pl
from jax.experimental.pallas import tpu as pltpu

D_MODEL = 1024
N_META = 16
CHUNK = 64
NORM_EPS = 1e-6
GLA_HEADS = 4
GLA_DK = 128
GLA_DV = 256
GLA_KEY = GLA_HEADS * GLA_DK
GLA_VAL = GLA_HEADS * GLA_DV
GLA_GATE_RANK = 16
GLA_GATE_NORM = 16.0
GLA_LOG_DECAY_MIN = -1.0
DN_HEADS = 8
DN_DK = 128
DN_DV = 128
DN_KEY = DN_HEADS * DN_DK
DN_VAL = DN_HEADS * DN_DV
DN_CONV = 4
D_FF = 4 * D_MODEL

MAIN_COLS = 2 * GLA_KEY + 2 * GLA_VAL + (2 * DN_KEY + DN_VAL) + DN_VAL + 2 * D_MODEL
LANES = 128
SUBLANES = 8
SMALL_COLS = LANES
LR_LANE0 = 0
DB_LANE0 = GLA_GATE_RANK
DA_LANE0 = GLA_GATE_RANK + DN_HEADS

VMEM_LIMIT = 56 * 1024 * 1024
BF16 = jnp.bfloat16
F32 = jnp.float32

_NT = (((1,), (1,)), ((), ()))
_TN = (((0,), (0,)), ((), ()))


def _dot(a, b):
    return jnp.dot(a, b, preferred_element_type=F32)


def _dot_nt(a, b):
    return lax.dot_general(a, b, _NT, preferred_element_type=F32)


def _dot_tn(a, b):
    return lax.dot_general(a, b, _TN, preferred_element_type=F32)


def _bf(x):
    return x.astype(BF16)


def _split_bf16(x, parts):
    out = []
    r = x
    for _ in range(parts):
        p = r.astype(BF16)
        out.append(p)
        r = r - p.astype(F32)
    return out


def _dot_exact_left(m_bf16, x, parts=3):
    acc = None
    for p in _split_bf16(x, parts):
        t = _dot(m_bf16, p)
        acc = t if acc is None else acc + t
    return acc


def _dot_exact_right(x, m_bf16, parts=3):
    acc = None
    for p in _split_bf16(x, parts):
        t = _dot(p, m_bf16)
        acc = t if acc is None else acc + t
    return acc


def _softplus(x):
    return jnp.maximum(x, 0.0) + jnp.log1p(jnp.exp(-jnp.abs(x)))


def _silu(x):
    return x * jax.nn.sigmoid(x)


def _rms(x, w):
    ms = jnp.mean(x * x, axis=-1, keepdims=True)
    return (x * lax.rsqrt(ms + NORM_EPS)) * w


def _row_tile(total_rows, target):
    n = total_rows // CHUNK
    best = 1
    for d in range(1, n + 1):
        if n % d == 0 and d * CHUNK <= target:
            best = d
    return best * CHUNK


def _col_tile(total_cols, target):
    n = total_cols // LANES
    best = 1
    for d in range(1, n + 1):
        if n % d == 0 and d * LANES <= target:
            best = d
    return best * LANES


def _rows_per_step(batch, target):
    return max(d for d in range(1, target + 1) if batch % d == 0)


def _const_spec(shape):
    nd = len(shape)
    return pl.BlockSpec(shape, lambda *_: (0,) * nd, pipeline_mode=pl.Buffered(1))


def _chunk_spec(bb, cols, col_block):
    return pl.BlockSpec((None, bb, CHUNK, cols), lambda s, n: (s, 0, n, col_block))


def _inproj_kernel(h_ref, nw_ref, w_ref, ws_ref, o_ref, os_ref, xn_ref):
    @pl.when(pl.program_id(1) == 0)
    def _():
        xn = _bf(_rms(h_ref[...], nw_ref[...]))
        xn_ref[...] = xn
        os_ref[...] = _dot(xn, ws_ref[...])

    o_ref[...] = _bf(_dot(xn_ref[...], w_ref[...]))


def _inproj(h, norm_w, w_main, w_small):
    rows, d = h.shape
    tm = _row_tile(rows, 1280)
    tn = _col_tile(MAIN_COLS, 1536)
    return pl.pallas_call(
        _inproj_kernel,
        out_shape=(jax.ShapeDtypeStruct((rows, MAIN_COLS), BF16),
                   jax.ShapeDtypeStruct((rows, SMALL_COLS), F32)),
        grid=(rows // tm, MAIN_COLS // tn),
        in_specs=[
            pl.BlockSpec((tm, d), lambda i, j: (i, 0)),
            _const_spec((1, d)),
            pl.BlockSpec((d, tn), lambda i, j: (0, j)),
            _const_spec((d, SMALL_COLS)),
        ],
        out_specs=(pl.BlockSpec((tm, tn), lambda i, j: (i, j)),
                   pl.BlockSpec((tm, SMALL_COLS), lambda i, j: (i, 0))),
        scratch_shapes=[pltpu.VMEM((tm, d), BF16)],
        compiler_params=pltpu.CompilerParams(
            dimension_semantics=("arbitrary", "arbitrary"), vmem_limit_bytes=VMEM_LIMIT),
        name="inproj",
    )(h, norm_w, w_main, w_small)


def _gla_kernel(q_ref, k_ref, v_ref, z_ref, sm_ref, wg_ref, bg_ref, nw_ref, o_ref, st_ref, *, pad):
    bb = q_ref.shape[0]
    n = pl.program_id(1)
    c = CHUNK

    @pl.when(n == 0)
    def _():
        st_ref[...] = jnp.zeros_like(st_ref)

    row = lax.broadcasted_iota(jnp.int32, (c, 1), 0)
    valid = jnp.logical_or(n > 0, row >= pad)
    ri = lax.broadcasted_iota(jnp.int32, (c, c), 0)
    ci = lax.broadcasted_iota(jnp.int32, (c, c), 1)
    incl = ri >= ci
    tri = _bf(jnp.where(incl, 1.0, 0.0))
    rows = range(bb)
    probs = [(i, h) for i in rows for h in range(GLA_HEADS)]
    ks = lambda h: slice(h * GLA_DK, (h + 1) * GLA_DK)
    vs = lambda h: slice(h * GLA_DV, (h + 1) * GLA_DV)

    logit = [_dot(_bf(sm_ref[i]), wg_ref[...]) + bg_ref[...] for i in rows]
    g = []
    for i in rows:
        log_sig = jnp.minimum(logit[i], 0.0) - jnp.log1p(jnp.exp(-jnp.abs(logit[i])))
        g.append(jnp.where(valid, jnp.maximum(log_sig / GLA_GATE_NORM, GLA_LOG_DECAY_MIN), 0.0))
    cum = [_dot_exact_left(tri, g[i]) for i in rows]
    cl = [cum[i][c - 1:c, :] for i in rows]
    qe = [_bf((q_ref[i].astype(F32) * jnp.exp(cum[i])) * (GLA_DK ** -0.5)) for i in rows]
    kf = [k_ref[i].astype(F32) for i in rows]
    ke = [_bf(kf[i] * jnp.exp(-cum[i])) for i in rows]
    kd = [_bf(kf[i] * jnp.exp(cl[i] - cum[i])) for i in rows]
    dec = [jnp.exp(cl[i]) for i in rows]

    st = [st_ref[i * GLA_HEADS + h] for i, h in probs]
    scores = [jnp.where(incl, _dot_nt(qe[i][:, ks(h)], ke[i][:, ks(h)]), 0.0) for i, h in probs]
    inter = [_dot_nt(qe[i][:, ks(h)], _bf(st[p])) for p, (i, h) in enumerate(probs)]
    kv = [_dot_tn(v_ref[i, :, vs(h)], kd[i][:, ks(h)]) for i, h in probs]
    o = [_dot(_bf(scores[p]), v_ref[i, :, vs(h)]) + inter[p] for p, (i, h) in enumerate(probs)]
    for p, (i, h) in enumerate(probs):
        st_ref[i * GLA_HEADS + h] = st[p] * dec[i][:, ks(h)] + kv[p]
        y = _rms(o[p], nw_ref[...])
        o_ref[i, :, vs(h)] = _bf(y * _silu(z_ref[i, :, vs(h)].astype(F32)))


def _gla(proj, small, wg, bg, nw, bb, pad):
    steps, _, lp, _ = proj.shape
    kern = functools.partial(_gla_kernel, pad=pad)
    return pl.pallas_call(
        kern,
        out_shape=jax.ShapeDtypeStruct((steps, bb, lp, GLA_VAL), BF16),
        grid=(steps, lp // CHUNK),
        in_specs=[
            _chunk_spec(bb, GLA_KEY, 0),
            _chunk_spec(bb, GLA_KEY, 1),
            _chunk_spec(bb, GLA_VAL, 1),
            _chunk_spec(bb, GLA_VAL, 2),
            _chunk_spec(bb, SMALL_COLS, 0),
            _const_spec((SMALL_COLS, GLA_KEY)),
            _const_spec((1, GLA_KEY)),
            _const_spec((1, GLA_DV)),
        ],
        out_specs=_chunk_spec(bb, GLA_VAL, 0),
        scratch_shapes=[pltpu.VMEM((bb * GLA_HEADS, GLA_DV, GLA_DK), F32)],
        compiler_params=pltpu.CompilerParams(
            dimension_semantics=("arbitrary", "arbitrary"), vmem_limit_bytes=VMEM_LIMIT),
        name="gla",
    )(proj, proj, proj, proj, small, wg, bg, nw)


def _unit_lower_inverse(a_list, ri, ci):
    eye = jnp.where(ri == ci, 1.0, 0.0)
    base = 8
    blk = lambda idx, size: lax.shift_right_logical(idx, size.bit_length() - 1)
    dmask = blk(ri, base) == blk(ci, base)
    dm = [jnp.where(dmask, a, 0.0) for a in a_list]
    d = [_bf(x) for x in dm]
    d2 = [_bf(_dot(x, x)) for x in d]
    t = [eye - x for x in dm]
    t = [x + _dot(_bf(x), y) for x, y in zip(t, d2)]
    d4 = [_bf(_dot(y, y)) for y in d2]
    t = [x + _dot(_bf(x), y) for x, y in zip(t, d4)]
    b = base
    while b < CHUNK:
        lmask = jnp.logical_and(blk(ri, 2 * b) == blk(ci, 2 * b), blk(ri, b) != blk(ci, b))
        l = [_bf(jnp.where(lmask, a, 0.0)) for a in a_list]
        tb = [_bf(x) for x in t]
        m = [_bf(_dot(x, y)) for x, y in zip(tb, l)]
        t = [x - _dot(y, z) for x, y, z in zip(t, m, tb)]
        b *= 2
    return t


def _dn_kernel(qkv_ref, z_ref, sm_ref, cw_ref, alog_ref, dtb_ref, nw_ref, eb_ref, ea_ref, ea64_ref,
               o_ref, xbuf, st_ref, *, pad):
    bb = qkv_ref.shape[0]
    n = pl.program_id(1)
    c = CHUNK
    tail = SUBLANES

    @pl.when(n == 0)
    def _():
        xbuf[:, 0:tail, :] = jnp.zeros((bb, tail, xbuf.shape[2]), F32)
        st_ref[...] = jnp.zeros_like(st_ref)

    row = lax.broadcasted_iota(jnp.int32, (c, 1), 0)
    valid = jnp.logical_or(n > 0, row >= pad)
    ri = lax.broadcasted_iota(jnp.int32, (c, c), 0)
    ci = lax.broadcasted_iota(jnp.int32, (c, c), 1)
    incl = ri >= ci
    strict = ri > ci
    tri = _bf(jnp.where(incl, 1.0, 0.0))
    rows = range(bb)
    probs = [(i, h) for i in rows for h in range(DN_HEADS)]
    hs = lambda h: slice(h * DN_DK, (h + 1) * DN_DK)

    sm = [sm_ref[i] for i in rows]
    beta_blk = [jnp.where(valid, jax.nn.sigmoid(sm[i]), 0.0) for i in rows]
    la_blk = [jnp.where(valid, -jnp.exp(alog_ref[...]) * _softplus(sm[i] + dtb_ref[...]), 0.0)
              for i in rows]
    beta128 = [_dot_exact_right(beta_blk[i], eb_ref[...], parts=2) for i in rows]
    gam_blk = [_dot_exact_left(tri, la_blk[i]) for i in rows]
    la64 = [_dot_exact_right(la_blk[i], ea64_ref[...]) for i in rows]
    gam128 = [_dot_exact_right(gam_blk[i], ea_ref[...]) for i in rows]
    sj = lax.broadcasted_iota(jnp.int32, (c, DN_HEADS * c), 0)
    ss = jnp.bitwise_and(lax.broadcasted_iota(jnp.int32, (c, DN_HEADS * c), 1), c - 1)
    dsum = [_dot_exact_left(tri, jnp.where(sj > ss, la64[i], 0.0)) for i in rows]
    decay_all = [jnp.where(sj >= ss, jnp.exp(dsum[i]), 0.0) for i in rows]

    for i in rows:
        xbuf[i, tail:tail + c, :] = qkv_ref[i].astype(F32)
    scale = DN_DK ** -0.5

    def conv_silu(i, col0):
        cs = slice(col0, col0 + LANES)
        y = cw_ref[DN_CONV - 1:DN_CONV, cs] * xbuf[i, tail:tail + c, cs]
        for j in range(DN_CONV - 1):
            off = tail - (DN_CONV - 1) + j
            y = y + cw_ref[j:j + 1, cs] * xbuf[i, off:off + c, cs]
        return _silu(y)

    def l2n(x):
        return x * lax.rsqrt(jnp.sum(x * x, axis=-1, keepdims=True) + NORM_EPS)

    q = [l2n(conv_silu(i, h * DN_DK)) for i, h in probs]
    k = [l2n(conv_silu(i, DN_KEY + h * DN_DK)) for i, h in probs]
    v = [conv_silu(i, 2 * DN_KEY + h * DN_DV) for i, h in probs]
    for i in rows:
        xbuf[i, 0:tail, :] = xbuf[i, c:c + tail, :]

    b_h = [beta128[i][:, hs(h)] for i, h in probs]
    g_h = [gam128[i][:, hs(h)] for i, h in probs]
    g_last = [x[c - 1:c, :] for x in g_h]
    eg = [jnp.exp(x) for x in g_h]
    dm = [decay_all[i][:, h * c:(h + 1) * c] for i, h in probs]
    kb = [x * y for x, y in zip(k, b_h)]
    k_bf = [_bf(x) for x in k]
    qs_bf = [_bf(x * scale) for x in q]
    qd = [_bf((x * scale) * y) for x, y in zip(q, eg)]
    kd = [_bf(x * jnp.exp(y - z)) for x, y, z in zip(k, g_last, g_h)]
    rhs = [_bf(jnp.concatenate([x * y, z * w], axis=1)) for x, y, z, w in zip(v, b_h, kb, eg)]

    a_kk = [jnp.where(strict, _dot_nt(_bf(x), y) * z, 0.0) for x, y, z in zip(kb, k_bf, dm)]
    a_qk = [_bf(_dot_nt(x, y) * z) for x, y, z in zip(qs_bf, k_bf, dm)]
    t_bf = [_bf(x) for x in _unit_lower_inverse(a_kk, ri, ci)]
    uw = [_dot(x, y) for x, y in zip(t_bf, rhs)]

    st = [st_ref[p] for p in range(len(probs))]
    st_bf = [_bf(x) for x in st]
    v_new = [_bf(x[:, :DN_DV] - _dot(_bf(x[:, DN_DV:]), y)) for x, y in zip(uw, st_bf)]
    o = [_dot(x, y) + _dot(z, w) for x, y, z, w in zip(qd, st_bf, a_qk, v_new)]
    upd = [_dot_tn(x, y) for x, y in zip(kd, v_new)]
    for p, (i, h) in enumerate(probs):
        st_ref[p] = st[p] * jnp.exp(g_last[p]) + upd[p]
        y = _rms(o[p], nw_ref[...])
        o_ref[i, :, hs(h)] = _bf(y * _silu(z_ref[i, :, hs(h)].astype(F32)))


def _dn(proj, small, cw, alog, dtb, nw, eb, ea, ea64, bb, pad):
    steps, _, lp, _ = proj.shape
    c = CHUNK
    qkv_cols = 2 * DN_KEY + DN_VAL
    kern = functools.partial(_dn_kernel, pad=pad)
    return pl.pallas_call(
        kern,
        out_shape=jax.ShapeDtypeStruct((steps, bb, lp, DN_VAL), BF16),
        grid=(steps, lp // c),
        in_specs=[
            _chunk_spec(bb, qkv_cols, 1),
            _chunk_spec(bb, DN_VAL, 6),
            _chunk_spec(bb, SMALL_COLS, 0),
            _const_spec((DN_CONV, qkv_cols)),
            _const_spec((1, SMALL_COLS)),
            _const_spec((1, SMALL_COLS)),
            _const_spec((1, DN_DV)),
            _const_spec((SMALL_COLS, DN_HEADS * LANES)),
            _const_spec((SMALL_COLS, DN_HEADS * LANES)),
            _const_spec((SMALL_COLS, DN_HEADS * c)),
        ],
        out_specs=_chunk_spec(bb, DN_VAL, 0),
        scratch_shapes=[pltpu.VMEM((bb, c + SUBLANES, qkv_cols), F32),
                        pltpu.VMEM((bb * DN_HEADS, DN_DK, DN_DV), F32)],
        compiler_params=pltpu.CompilerParams(
            dimension_semantics=("arbitrary", "arbitrary"), vmem_limit_bytes=VMEM_LIMIT),
        name="deltanet",
    )(proj, proj, small, cw, alog, dtb, nw, eb, ea, ea64)


def _merge_kernel(og_ref, od_ref, gg_ref, gd_ref, h_ref, wbg_ref, wbd_ref, wo_ref, o_ref):
    pg = _dot(og_ref[...], wbg_ref[...])
    pd = _dot(od_ref[...], wbd_ref[...])
    merged = (jax.nn.sigmoid(gg_ref[...].astype(F32)) * pg
              + jax.nn.sigmoid(gd_ref[...].astype(F32)) * pd)
    o_ref[...] = h_ref[...] + _dot(_bf(merged), wo_ref[...])


def _merge(o_gla, o_dn, proj, h, wbg, wbd, wo):
    rows, d = h.shape
    tm = _row_tile(rows, 640)
    row_spec = lambda cols, j: pl.BlockSpec((tm, cols), lambda i: (i, j))
    return pl.pallas_call(
        _merge_kernel,
        out_shape=jax.ShapeDtypeStruct((rows, d), F32),
        grid=(rows // tm,),
        in_specs=[
            row_spec(GLA_VAL, 0),
            row_spec(DN_VAL, 0),
            row_spec(d, 7),
            row_spec(d, 8),
            row_spec(d, 0),
            _const_spec((GLA_VAL, d)),
            _const_spec((DN_VAL, d)),
            _const_spec((d, d)),
        ],
        out_specs=row_spec(d, 0),
        compiler_params=pltpu.CompilerParams(
            dimension_semantics=("arbitrary",), vmem_limit_bytes=VMEM_LIMIT),
        name="merge",
    )(o_gla, o_dn, proj, proj, h, wbg, wbd, wo)


def _mlp_kernel(h_ref, nw_ref, wu_ref, wd_ref, o_ref, *, ff_tile):
    x = h_ref[...]
    xn = _bf(_rms(x, nw_ref[...]))
    acc = x
    for f in range(0, D_FF, ff_tile):
        up = jnp.maximum(_dot(xn, wu_ref[:, f:f + ff_tile]), 0.0)
        acc = acc + _dot(_bf(up * up), wd_ref[f:f + ff_tile, :])
    o_ref[...] = acc


def _mlp(h, norm_w, wu, wd):
    rows, d = h.shape
    tm = _row_tile(rows, 640)
    kern = functools.partial(_mlp_kernel, ff_tile=1024)
    return pl.pallas_call(
        kern,
        out_shape=jax.ShapeDtypeStruct((rows, d), F32),
        grid=(rows // tm,),
        in_specs=[
            pl.BlockSpec((tm, d), lambda i: (i, 0)),
            _const_spec((1, d)),
            _const_spec((d, D_FF)),
            _const_spec((D_FF, d)),
        ],
        out_specs=pl.BlockSpec((tm, d), lambda i: (i, 0)),
        compiler_params=pltpu.CompilerParams(
            dimension_semantics=("arbitrary",), vmem_limit_bytes=VMEM_LIMIT),
        name="mlp",
    )(h, norm_w, wu, wd)


def _norm_kernel(h_ref, nw_ref, o_ref):
    o_ref[...] = _rms(h_ref[...], nw_ref[...])


def _final_norm(h, norm_w):
    rows, d = h.shape
    tm = _row_tile(rows, 1280)
    return pl.pallas_call(
        _norm_kernel,
        out_shape=jax.ShapeDtypeStruct((rows, d), F32),
        grid=(rows // tm,),
        in_specs=[pl.BlockSpec((tm, d), lambda i: (i, 0)), _const_spec((1, d))],
        out_specs=pl.BlockSpec((tm, d), lambda i: (i, 0)),
        compiler_params=pltpu.CompilerParams(
            dimension_semantics=("arbitrary",), vmem_limit_bytes=VMEM_LIMIT),
        name="final_norm",
    )(h, norm_w)


def _lane_expand_matrix(lane0, width):
    r = jnp.arange(SMALL_COLS)[:, None]
    col_head = jnp.arange(DN_HEADS * width)[None, :] // width
    return (r == lane0 + col_head).astype(BF16)


def _pad_lanes(v, lane0):
    return jnp.zeros((1, SMALL_COLS), F32).at[0, lane0:lane0 + v.shape[0]].set(v.astype(F32))


def kernel(x, meta_tokens, mixer_norm_w, w_in, gla_w_gate_up, gla_b_gate, gla_norm_w, dn_conv_w,
           dn_a_log, dn_dt_bias, dn_norm_w, w_branch_gla, w_branch_dn, w_out, mlp_norm_w,
           w_mlp_up, w_mlp_down, final_norm_w):
    batch, seq, d = x.shape
    depth = w_in.shape[0]
    pad = (-(N_META + seq)) % CHUNK
    lp = pad + N_META + seq
    nc = lp // CHUNK
    meta = jnp.broadcast_to(meta_tokens[None].astype(x.dtype), (batch, N_META, d))
    h = jnp.concatenate([jnp.zeros((batch, pad, d), x.dtype), meta, x], axis=1).reshape(batch * lp, d)

    o_lr = 2 * GLA_KEY + GLA_VAL
    o_z = o_lr + GLA_GATE_RANK
    o_db = o_z + GLA_VAL + (2 * DN_KEY + DN_VAL) + DN_VAL
    o_gate = o_db + 2 * DN_HEADS

    eb = _lane_expand_matrix(DB_LANE0, LANES)
    ea = _lane_expand_matrix(DA_LANE0, LANES)
    ea64 = _lane_expand_matrix(DA_LANE0, CHUNK)

    bb = _rows_per_step(batch, 2)
    by_rows = lambda a: a.reshape(batch // bb, bb, lp, a.shape[-1])
    flat = lambda a: a.reshape(batch * lp, a.shape[-1])

    for l in range(depth):
        w = w_in[l]
        w_main = jnp.concatenate([w[:, :o_lr], w[:, o_z:o_db], w[:, o_gate:]], axis=1).astype(BF16)
        w_small = jnp.concatenate(
            [w[:, o_lr:o_z], w[:, o_db:o_gate],
             jnp.zeros((d, SMALL_COLS - GLA_GATE_RANK - 2 * DN_HEADS), w.dtype)], axis=1).astype(BF16)
        wg = jnp.zeros((SMALL_COLS, GLA_KEY), F32).at[LR_LANE0:LR_LANE0 + GLA_GATE_RANK].set(
            gla_w_gate_up[l]).astype(BF16)

        proj, small = _inproj(h, mixer_norm_w[l][None, :], w_main, w_small)
        o_gla = flat(_gla(by_rows(proj), by_rows(small), wg, gla_b_gate[l][None, :],
                          gla_norm_w[l][None, :], bb, pad))
        o_dn = flat(_dn(by_rows(proj), by_rows(small), dn_conv_w[l],
                        _pad_lanes(dn_a_log[l], DA_LANE0), _pad_lanes(dn_dt_bias[l], DA_LANE0),
                        dn_norm_w[l][None, :], eb, ea, ea64, bb, pad))
        h = _merge(o_gla, o_dn, proj, h, w_branch_gla[l].astype(BF16), w_branch_dn[l].astype(BF16),
                   w_out[l].astype(BF16))
        h = _mlp(h, mlp_norm_w[l][None, :], w_mlp_up[l].astype(BF16), w_mlp_down[l].astype(BF16))

    out = _final_norm(h, final_norm_w[None, :])
    return out.reshape(batch, lp, d)[:, pad + N_META:]
```

```python
import functools

import jax
import jax.numpy as jnp
from jax import lax
from jax.experimental import pallas as pl
from jax.experimental.pallas import tpu as pltpu

D_MODEL = 1024
N_META = 16
CHUNK = 64
NORM_EPS = 1e-6
GLA_HEADS = 4
GLA_DK = 128
GLA_DV = 256
GLA_KEY = GLA_HEADS * GLA_DK
GLA_VAL = GLA_HEADS * GLA_DV
GLA_GATE_RANK = 16
GLA_GATE_NORM = 16.0
GLA_LOG_DECAY_MIN = -1.0
DN_HEADS = 8
DN_DK = 128
DN_DV = 128
DN_KEY = DN_HEADS * DN_DK
DN_VAL = DN_HEADS * DN_DV
DN_QKV = 2 * DN_KEY + DN_VAL
DN_CONV = 4
D_FF = 4 * D_MODEL

MAIN_COLS = 2 * GLA_KEY + 2 * GLA_VAL + DN_QKV + DN_VAL + 2 * D_MODEL
LANES = 128
SUBLANES = 8
SMALL_COLS = LANES
LR_LANE0 = 0
DB_LANE0 = GLA_GATE_RANK
DA_LANE0 = GLA_GATE_RANK + DN_HEADS

VMEM_LIMIT = 56 * 1024 * 1024
BF16 = jnp.bfloat16
F32 = jnp.float32

_NT = (((1,), (1,)), ((), ()))
_TN = (((0,), (0,)), ((), ()))


def _dot(a, b):
    return jnp.dot(a, b, preferred_element_type=F32)


def _dot_nt(a, b):
    return lax.dot_general(a, b, _NT, preferred_element_type=F32)


def _dot_tn(a, b):
    return lax.dot_general(a, b, _TN, preferred_element_type=F32)


def _bf(x):
    return x.astype(BF16)


def _split_bf16(x, parts):
    out = []
    r = x
    for _ in range(parts):
        p = r.astype(BF16)
        out.append(p)
        r = r - p.astype(F32)
    return out


def _dot_exact_left(m_bf16, x, parts=3):
    acc = None
    for p in _split_bf16(x, parts):
        t = _dot(m_bf16, p)
        acc = t if acc is None else acc + t
    return acc


def _dot_exact_right(x, m_bf16, parts=3):
    acc = None
    for p in _split_bf16(x, parts):
        t = _dot(p, m_bf16)
        acc = t if acc is None else acc + t
    return acc


def _softplus(x):
    return jnp.maximum(x, 0.0) + jnp.log1p(jnp.exp(-jnp.abs(x)))


def _silu(x):
    return x * jax.nn.sigmoid(x)


def _rms(x, w):
    ms = jnp.mean(x * x, axis=-1, keepdims=True)
    return (x * lax.rsqrt(ms + NORM_EPS)) * w


def _row_tile(total_rows, target):
    n = total_rows // CHUNK
    best = 1
    for d in range(1, n + 1):
        if n % d == 0 and d * CHUNK <= target:
            best = d
    return best * CHUNK


def _col_tile(total_cols, target):
    n = total_cols // LANES
    best = 1
    for d in range(1, n + 1):
        if n % d == 0 and d * LANES <= target:
            best = d
    return best * LANES


def _rows_per_step(batch, target):
    return max(d for d in range(1, target + 1) if batch % d == 0)


def _const_spec(shape):
    nd = len(shape)
    return pl.BlockSpec(shape, lambda *_: (0,) * nd, pipeline_mode=pl.Buffered(1))


def _chunk_spec(bb, cols, col_block):
    return pl.BlockSpec((None, bb, CHUNK, cols), lambda s, n: (s, 0, n, col_block))


def _run(gen):
    for _ in gen:
        pass


def _inproj_kernel(h_ref, nw_ref, w_ref, ws_ref, o_ref, os_ref, *, col_tile):
    xn = _bf(_rms(h_ref[...], nw_ref[...]))
    os_ref[...] = _dot(xn, ws_ref[...])
    for c0 in range(0, MAIN_COLS, col_tile):
        o_ref[:, c0:c0 + col_tile] = _bf(_dot(xn, w_ref[:, c0:c0 + col_tile]))


def _inproj(h, norm_w, w_main, w_small):
    rows, d = h.shape
    tm = _row_tile(rows, 640)
    kern = functools.partial(_inproj_kernel, col_tile=_col_tile(MAIN_COLS, 1024))
    return pl.pallas_call(
        kern,
        out_shape=(jax.ShapeDtypeStruct((rows, MAIN_COLS), BF16),
                   jax.ShapeDtypeStruct((rows, SMALL_COLS), F32)),
        grid=(rows // tm,),
        in_specs=[
            pl.BlockSpec((tm, d), lambda i: (i, 0)),
            _const_spec((1, d)),
            _const_spec((d, MAIN_COLS)),
            _const_spec((d, SMALL_COLS)),
        ],
        out_specs=(pl.BlockSpec((tm, MAIN_COLS), lambda i: (i, 0)),
                   pl.BlockSpec((tm, SMALL_COLS), lambda i: (i, 0))),
        compiler_params=pltpu.CompilerParams(
            dimension_semantics=("arbitrary",), vmem_limit_bytes=VMEM_LIMIT),
        name="inproj",
    )(h, norm_w, w_main, w_small)


def _gla_stages(rows, valid, incl, tri, q_ref, k_ref, v_ref, z_ref, sm_ref, wg_ref, bg_ref, nw_ref,
                o_ref, st_ref):
    c = CHUNK
    probs = [(i, h) for i in rows for h in range(GLA_HEADS)]
    ks = lambda h: slice(h * GLA_DK, (h + 1) * GLA_DK)
    vs = lambda h: slice(h * GLA_DV, (h + 1) * GLA_DV)

    logit = {i: _dot(_bf(sm_ref[i]), wg_ref[...]) + bg_ref[...] for i in rows}
    yield
    g = {}
    for i in rows:
        log_sig = jnp.minimum(logit[i], 0.0) - jnp.log1p(jnp.exp(-jnp.abs(logit[i])))
        g[i] = jnp.where(valid, jnp.maximum(log_sig / GLA_GATE_NORM, GLA_LOG_DECAY_MIN), 0.0)
    cum = {i: _dot_exact_left(tri, g[i]) for i in rows}
    yield
    qe, ke, kd, dec = {}, {}, {}, {}
    for i in rows:
        cl = cum[i][c - 1:c, :]
        qe[i] = _bf((q_ref[i].astype(F32) * jnp.exp(cum[i])) * (GLA_DK ** -0.5))
        kf = k_ref[i].astype(F32)
        ke[i] = _bf(kf * jnp.exp(-cum[i]))
        kd[i] = _bf(kf * jnp.exp(cl - cum[i]))
        dec[i] = jnp.exp(cl)
        yield

    slot = [i * GLA_HEADS + h for i, h in probs]
    st = [st_ref[s] for s in slot]
    scores = [jnp.where(incl, _dot_nt(qe[i][:, ks(h)], ke[i][:, ks(h)]), 0.0) for i, h in probs]
    yield
    inter = [_dot_nt(qe[i][:, ks(h)], _bf(st[p])) for p, (i, h) in enumerate(probs)]
    yield
    kv = [_dot_tn(v_ref[i, :, vs(h)], kd[i][:, ks(h)]) for i, h in probs]
    yield
    o = [_dot(_bf(scores[p]), v_ref[i, :, vs(h)]) + inter[p] for p, (i, h) in enumerate(probs)]
    yield
    for p, (i, h) in enumerate(probs):
        st_ref[slot[p]] = st[p] * dec[i][:, ks(h)] + kv[p]
        y = _rms(o[p], nw_ref[...])
        o_ref[i, :, vs(h)] = _bf(y * _silu(z_ref[i, :, vs(h)].astype(F32)))
        yield


def _unit_lower_inverse_stages(a_list, ri, ci):
    eye = jnp.where(ri == ci, 1.0, 0.0)
    base = 8
    blk = lambda idx, size: lax.shift_right_logical(idx, size.bit_length() - 1)
    dmask = blk(ri, base) == blk(ci, base)
    dm = [jnp.where(dmask, a, 0.0) for a in a_list]
    d = [_bf(x) for x in dm]
    d2 = [_bf(_dot(x, x)) for x in d]
    yield
    t = [eye - x for x in dm]
    t = [x + _dot(_bf(x), y) for x, y in zip(t, d2)]
    yield
    d4 = [_bf(_dot(y, y)) for y in d2]
    yield
    t = [x + _dot(_bf(x), y) for x, y in zip(t, d4)]
    yield
    b = base
    while b < CHUNK:
        lmask = jnp.logical_and(blk(ri, 2 * b) == blk(ci, 2 * b), blk(ri, b) != blk(ci, b))
        l = [_bf(jnp.where(lmask, a, 0.0)) for a in a_list]
        tb = [_bf(x) for x in t]
        m = [_bf(_dot(x, y)) for x, y in zip(tb, l)]
        yield
        t = [x - _dot(y, z) for x, y, z in zip(t, m, tb)]
        yield
        b *= 2
    return t


def _dn_prepare_stages(rows, out, valid, tri, sj, ss, conv_silu, sm_ref, alog_ref, dtb_ref, eb_ref,
                       ea_ref, ea64_ref):
    c = CHUNK
    hs = lambda h: slice(h * DN_DK, (h + 1) * DN_DK)
    scale = DN_DK ** -0.5

    def l2n(x):
        return x * lax.rsqrt(jnp.sum(x * x, axis=-1, keepdims=True) + NORM_EPS)

    sm = {i: sm_ref[i] for i in rows}
    beta_blk = {i: jnp.where(valid, jax.nn.sigmoid(sm[i]), 0.0) for i in rows}
    la_blk = {i: jnp.where(valid, -jnp.exp(alog_ref[...]) * _softplus(sm[i] + dtb_ref[...]), 0.0)
              for i in rows}
    beta128 = {i: _dot_exact_right(beta_blk[i], eb_ref[...], parts=2) for i in rows}
    gam_blk = {i: _dot_exact_left(tri, la_blk[i]) for i in rows}
    la64 = {i: _dot_exact_right(la_blk[i], ea64_ref[...]) for i in rows}
    yield
    gam128 = {i: _dot_exact_right(gam_blk[i], ea_ref[...]) for i in rows}
    dsum = {i: _dot_exact_left(tri, jnp.where(sj > ss, la64[i], 0.0)) for i in rows}
    decay_all = {i: jnp.where(sj >= ss, jnp.exp(dsum[i]), 0.0) for i in rows}
    yield

    for key in ("probs", "kb", "k", "qs", "qd", "kd", "rhs", "dm", "dec"):
        out[key] = []
    for i in rows:
        for h in range(DN_HEADS):
            q = l2n(conv_silu(i, h * DN_DK))
            k = l2n(conv_silu(i, DN_KEY + h * DN_DK))
            yield
            v = conv_silu(i, 2 * DN_KEY + h * DN_DV)
            b_h = beta128[i][:, hs(h)]
            g_h = gam128[i][:, hs(h)]
            g_last = g_h[c - 1:c, :]
            eg = jnp.exp(g_h)
            kb = k * b_h
            out["probs"].append((i, h))
            out["kb"].append(_bf(kb))
            out["k"].append(_bf(k))
            out["qs"].append(_bf(q * scale))
            out["qd"].append(_bf((q * scale) * eg))
            out["kd"].append(_bf(k * jnp.exp(g_last - g_h)))
            out["rhs"].append(_bf(jnp.concatenate([v * b_h, kb * eg], axis=1)))
            out["dm"].append(decay_all[i][:, h * c:(h + 1) * c])
            out["dec"].append(jnp.exp(g_last))
            yield


def _dn_solve_stages(a, ri, ci, strict, z_ref, nw_ref, o_ref, st_ref):
    hs = lambda h: slice(h * DN_DK, (h + 1) * DN_DK)
    probs = a["probs"]
    a_kk = [jnp.where(strict, _dot_nt(x, y) * z, 0.0) for x, y, z in zip(a["kb"], a["k"], a["dm"])]
    yield
    a_qk = [_bf(_dot_nt(x, y) * z) for x, y, z in zip(a["qs"], a["k"], a["dm"])]
    yield
    t = yield from _unit_lower_inverse_stages(a_kk, ri, ci)
    uw = [_dot(_bf(x), y) for x, y in zip(t, a["rhs"])]
    yield
    slot = [i * DN_HEADS + h for i, h in probs]
    st = [st_ref[s] for s in slot]
    st_bf = [_bf(x) for x in st]
    v_new = [_bf(x[:, :DN_DV] - _dot(_bf(x[:, DN_DV:]), y)) for x, y in zip(uw, st_bf)]
    yield
    o = [_dot(x, y) + _dot(z, w) for x, y, z, w in zip(a["qd"], st_bf, a_qk, v_new)]
    yield
    upd = [_dot_tn(x, y) for x, y in zip(a["kd"], v_new)]
    yield
    for p, (i, h) in enumerate(probs):
        st_ref[slot[p]] = st[p] * a["dec"][p] + upd[p]
        y = _rms(o[p], nw_ref[...])
        o_ref[i, :, hs(h)] = _bf(y * _silu(z_ref[i, :, hs(h)].astype(F32)))
        yield


def _gla_kernel(q_ref, k_ref, v_ref, z_ref, sm_ref, wg_ref, bg_ref, nw_ref, o_ref, st_ref, *, pad):
    bb = q_ref.shape[0]
    n = pl.program_id(1)
    c = CHUNK

    @pl.when(n == 0)
    def _():
        st_ref[...] = jnp.zeros_like(st_ref)

    row = lax.broadcasted_iota(jnp.int32, (c, 1), 0)
    valid = jnp.logical_or(n > 0, row >= pad)
    incl = (lax.broadcasted_iota(jnp.int32, (c, c), 0) >= lax.broadcasted_iota(jnp.int32, (c, c), 1))
    tri = _bf(jnp.where(incl, 1.0, 0.0))
    _run(_gla_stages(range(bb), valid, incl, tri, q_ref, k_ref, v_ref, z_ref, sm_ref, wg_ref,
                     bg_ref, nw_ref, o_ref, st_ref))


def _gla(proj, small, wg, bg, nw, bb, pad):
    steps, _, lp, _ = proj.shape
    kern = functools.partial(_gla_kernel, pad=pad)
    return pl.pallas_call(
        kern,
        out_shape=jax.ShapeDtypeStruct((steps, bb, lp, GLA_VAL), BF16),
        grid=(steps, lp // CHUNK),
        in_specs=[
            _chunk_spec(bb, GLA_KEY, 0),
            _chunk_spec(bb, GLA_KEY, 1),
            _chunk_spec(bb, GLA_VAL, 1),
            _chunk_spec(bb, GLA_VAL, 2),
            _chunk_spec(bb, SMALL_COLS, 0),
            _const_spec((SMALL_COLS, GLA_KEY)),
            _const_spec((1, GLA_KEY)),
            _const_spec((1, GLA_DV)),
        ],
        out_specs=_chunk_spec(bb, GLA_VAL, 0),
        scratch_shapes=[pltpu.VMEM((bb * GLA_HEADS, GLA_DV, GLA_DK), F32)],
        compiler_params=pltpu.CompilerParams(
            dimension_semantics=("arbitrary", "arbitrary"), vmem_limit_bytes=VMEM_LIMIT),
        name="gla",
    )(proj, proj, proj, proj, small, wg, bg, nw)


def _dn_kernel(qkv_ref, z_ref, sm_ref, cw_ref, alog_ref, dtb_ref, nw_ref, eb_ref, ea_ref, ea64_ref,
               o_ref, xbuf, st_ref, *, pad):
    bb = qkv_ref.shape[0]
    n = pl.program_id(1)
    c = CHUNK
    tail = SUBLANES

    @pl.when(n == 0)
    def _():
        xbuf[:, 0:tail, :] = jnp.zeros((bb, tail, xbuf.shape[2]), F32)
        st_ref[...] = jnp.zeros_like(st_ref)

    row = lax.broadcasted_iota(jnp.int32, (c, 1), 0)
    valid = jnp.logical_or(n > 0, row >= pad)
    ri = lax.broadcasted_iota(jnp.int32, (c, c), 0)
    ci = lax.broadcasted_iota(jnp.int32, (c, c), 1)
    incl = ri >= ci
    strict = ri > ci
    tri = _bf(jnp.where(incl, 1.0, 0.0))
    sj = lax.broadcasted_iota(jnp.int32, (c, DN_HEADS * c), 0)
    ss = jnp.bitwise_and(lax.broadcasted_iota(jnp.int32, (c, DN_HEADS * c), 1), c - 1)

    for i in range(bb):
        xbuf[i, tail:tail + c, :] = qkv_ref[i].astype(F32)

    def conv_silu(i, col0):
        cs = slice(col0, col0 + LANES)
        y = cw_ref[DN_CONV - 1:DN_CONV, cs] * xbuf[i, tail:tail + c, cs]
        for j in range(DN_CONV - 1):
            off = tail - (DN_CONV - 1) + j
            y = y + cw_ref[j:j + 1, cs] * xbuf[i, off:off + c, cs]
        return _silu(y)

    prepared = {}
    _run(_dn_prepare_stages(range(bb), prepared, valid, tri, sj, ss, conv_silu, sm_ref, alog_ref,
                            dtb_ref, eb_ref, ea_ref, ea64_ref))
    _run(_dn_solve_stages(prepared, ri, ci, strict, z_ref, nw_ref, o_ref, st_ref))

    for i in range(bb):
        xbuf[i, 0:tail, :] = xbuf[i, c:c + tail, :]


def _dn(proj, small, cw, alog, dtb, nw, eb, ea, ea64, bb, pad):
    steps, _, lp, _ = proj.shape
    c = CHUNK
    kern = functools.partial(_dn_kernel, pad=pad)
    return pl.pallas_call(
        kern,
        out_shape=jax.ShapeDtypeStruct((steps, bb, lp, DN_VAL), BF16),
        grid=(steps, lp // c),
        in_specs=[
            _chunk_spec(bb, DN_QKV, 1),
            _chunk_spec(bb, DN_VAL, 6),
            _chunk_spec(bb, SMALL_COLS, 0),
            _const_spec((DN_CONV, DN_QKV)),
            _const_spec((1, SMALL_COLS)),
            _const_spec((1, SMALL_COLS)),
            _const_spec((1, DN_DV)),
            _const_spec((SMALL_COLS, DN_HEADS * LANES)),
            _const_spec((SMALL_COLS, DN_HEADS * LANES)),
            _const_spec((SMALL_COLS, DN_HEADS * c)),
        ],
        out_specs=_chunk_spec(bb, DN_VAL, 0),
        scratch_shapes=[pltpu.VMEM((bb, c + SUBLANES, DN_QKV), F32),
                        pltpu.VMEM((bb * DN_HEADS, DN_DK, DN_DV), F32)],
        compiler_params=pltpu.CompilerParams(
            dimension_semantics=("arbitrary", "arbitrary"), vmem_limit_bytes=VMEM_LIMIT),
        name="deltanet",
    )(proj, proj, small, cw, alog, dtb, nw, eb, ea, ea64)


def _merge_kernel(og_ref, od_ref, gg_ref, gd_ref, h_ref, wbg_ref, wbd_ref, wo_ref, o_ref):
    pg = _dot(og_ref[...], wbg_ref[...])
    pd = _dot(od_ref[...], wbd_ref[...])
    merged = (jax.nn.sigmoid(gg_ref[...].astype(F32)) * pg
              + jax.nn.sigmoid(gd_ref[...].astype(F32)) * pd)
    o_ref[...] = h_ref[...] + _dot(_bf(merged), wo_ref[...])


def _merge(o_gla, o_dn, proj, h, wbg, wbd, wo):
    rows, d = h.shape
    tm = _row_tile(rows, 640)
    row_spec = lambda cols, j: pl.BlockSpec((tm, cols), lambda i: (i, j))
    return pl.pallas_call(
        _merge_kernel,
        out_shape=jax.ShapeDtypeStruct((rows, d), F32),
        grid=(rows // tm,),
        in_specs=[
            row_spec(GLA_VAL, 0),
            row_spec(DN_VAL, 0),
            row_spec(d, 7),
            row_spec(d, 8),
            row_spec(d, 0),
            _const_spec((GLA_VAL, d)),
            _const_spec((DN_VAL, d)),
            _const_spec((d, d)),
        ],
        out_specs=row_spec(d, 0),
        compiler_params=pltpu.CompilerParams(
            dimension_semantics=("arbitrary",), vmem_limit_bytes=VMEM_LIMIT),
        name="merge",
    )(o_gla, o_dn, proj, proj, h, wbg, wbd, wo)


def _mlp_kernel(h_ref, nw_ref, wu_ref, wd_ref, o_ref, *, ff_tile):
    x = h_ref[...]
    xn = _bf(_rms(x, nw_ref[...]))
    acc = x
    for f in range(0, D_FF, ff_tile):
        up = jnp.maximum(_dot(xn, wu_ref[:, f:f + ff_tile]), 0.0)
        acc = acc + _dot(_bf(up * up), wd_ref[f:f + ff_tile, :])
    o_ref[...] = acc


def _mlp(h, norm_w, wu, wd):
    rows, d = h.shape
    tm = _row_tile(rows, 640)
    kern = functools.partial(_mlp_kernel, ff_tile=1024)
    return pl.pallas_call(
        kern,
        out_shape=jax.ShapeDtypeStruct((rows, d), F32),
        grid=(rows // tm,),
        in_specs=[
            pl.BlockSpec((tm, d), lambda i: (i, 0)),
            _const_spec((1, d)),
            _const_spec((d, D_FF)),
            _const_spec((D_FF, d)),
        ],
        out_specs=pl.BlockSpec((tm, d), lambda i: (i, 0)),
        compiler_params=pltpu.CompilerParams(
            dimension_semantics=("arbitrary",), vmem_limit_bytes=VMEM_LIMIT),
        name="mlp",
    )(h, norm_w, wu, wd)


def _norm_kernel(h_ref, nw_ref, o_ref):
    o_ref[...] = _rms(h_ref[...], nw_ref[...])


def _final_norm(h, norm_w):
    rows, d = h.shape
    tm = _row_tile(rows, 1280)
    return pl.pallas_call(
        _norm_kernel,
        out_shape=jax.ShapeDtypeStruct((rows, d), F32),
        grid=(rows // tm,),
        in_specs=[pl.BlockSpec((tm, d), lambda i: (i, 0)), _const_spec((1, d))],
        out_specs=pl.BlockSpec((tm, d), lambda i: (i, 0)),
        compiler_params=pltpu.CompilerParams(
            dimension_semantics=("arbitrary",), vmem_limit_bytes=VMEM_LIMIT),
        name="final_norm",
    )(h, norm_w)


def _lane_expand_matrix(lane0, width):
    r = jnp.arange(SMALL_COLS)[:, None]
    col_head = jnp.arange(DN_HEADS * width)[None, :] // width
    return (r == lane0 + col_head).astype(BF16)


def _pad_lanes(v, lane0):
    return jnp.zeros((1, SMALL_COLS), F32).at[0, lane0:lane0 + v.shape[0]].set(v.astype(F32))


def kernel(x, meta_tokens, mixer_norm_w, w_in, gla_w_gate_up, gla_b_gate, gla_norm_w, dn_conv_w,
           dn_a_log, dn_dt_bias, dn_norm_w, w_branch_gla, w_branch_dn, w_out, mlp_norm_w,
           w_mlp_up, w_mlp_down, final_norm_w):
    batch, seq, d = x.shape
    depth = w_in.shape[0]
    pad = (-(N_META + seq)) % CHUNK
    lp = pad + N_META + seq
    meta = jnp.broadcast_to(meta_tokens[None].astype(x.dtype), (batch, N_META, d))
    h = jnp.concatenate([jnp.zeros((batch, pad, d), x.dtype), meta, x], axis=1).reshape(batch * lp, d)

    o_lr = 2 * GLA_KEY + GLA_VAL
    o_z = o_lr + GLA_GATE_RANK
    o_db = o_z + GLA_VAL + DN_QKV + DN_VAL
    o_gate = o_db + 2 * DN_HEADS

    eb = _lane_expand_matrix(DB_LANE0, LANES)
    ea = _lane_expand_matrix(DA_LANE0, LANES)
    ea64 = _lane_expand_matrix(DA_LANE0, CHUNK)

    bb_gla = _rows_per_step(batch, 8)
    bb_dn = _rows_per_step(batch, 4)
    by_rows = lambda a, bb: a.reshape(batch // bb, bb, lp, a.shape[-1])
    flat = lambda a: a.reshape(batch * lp, a.shape[-1])

    for l in range(depth):
        w = w_in[l]
        w_main = jnp.concatenate([w[:, :o_lr], w[:, o_z:o_db], w[:, o_gate:]], axis=1).astype(BF16)
        w_small = jnp.concatenate(
            [w[:, o_lr:o_z], w[:, o_db:o_gate],
             jnp.zeros((d, SMALL_COLS - GLA_GATE_RANK - 2 * DN_HEADS), w.dtype)], axis=1).astype(BF16)
        wg = jnp.zeros((SMALL_COLS, GLA_KEY), F32).at[LR_LANE0:LR_LANE0 + GLA_GATE_RANK].set(
            gla_w_gate_up[l]).astype(BF16)

        proj, small = _inproj(h, mixer_norm_w[l][None, :], w_main, w_small)
        o_gla = _gla(by_rows(proj, bb_gla), by_rows(small, bb_gla), wg, gla_b_gate[l][None, :],
                     gla_norm_w[l][None, :], bb_gla, pad)
        o_dn = _dn(by_rows(proj, bb_dn), by_rows(small, bb_dn), dn_conv_w[l],
                   _pad_lanes(dn_a_log[l], DA_LANE0), _pad_lanes(dn_dt_bias[l], DA_LANE0),
                   dn_norm_w[l][None, :], eb, ea, ea64, bb_dn, pad)
        h = _merge(flat(o_gla), flat(o_dn), proj, h, w_branch_gla[l].astype(BF16),
                   w_branch_dn[l].astype(BF16), w_out[l].astype(BF16))
        h = _mlp(h, mlp_norm_w[l][None, :], w_mlp_up[l].astype(BF16), w_mlp_down[l].astype(BF16))

    out = _final_norm(h, final_norm_w[None, :])
    return out.reshape(batch, lp, d)[:, pad + N_META:]
```

```python
import functools

import jax
import jax.numpy as jnp
from jax import lax
from jax.experimental import pallas as pl
from jax.experimental.pallas import tpu as pltpu

D_MODEL = 1024
N_META = 16
CHUNK = 64
NORM_EPS = 1e-6
GLA_HEADS = 4
GLA_DK = 128
GLA_DV = 256
GLA_KEY = GLA_HEADS * GLA_DK
GLA_VAL = GLA_HEADS * GLA_DV
GLA_GATE_RANK = 16
GLA_GATE_NORM = 16.0
GLA_LOG_DECAY_MIN = -1.0
DN_HEADS = 8
DN_DK = 128
DN_DV = 128
DN_KEY = DN_HEADS * DN_DK
DN_VAL = DN_HEADS * DN_DV
DN_QKV = 2 * DN_KEY + DN_VAL
DN_CONV = 4
D_FF = 4 * D_MODEL

COL_GQ = 0
COL_GK = COL_GQ + GLA_KEY
COL_GV = COL_GK + GLA_KEY
COL_GZ = COL_GV + GLA_VAL
COL_DQKV = COL_GZ + GLA_VAL
COL_DZ = COL_DQKV + DN_QKV
COL_GATES = COL_DZ + DN_VAL
MAIN_COLS = COL_GATES + 2 * D_MODEL
LANES = 128
SUBLANES = 8
SMALL_COLS = LANES
LR_LANE0 = 0
DB_LANE0 = GLA_GATE_RANK
DA_LANE0 = GLA_GATE_RANK + DN_HEADS

VMEM_LIMIT = 56 * 1024 * 1024
INPROJ_ROWS = 640
MERGE_ROWS = 1280
MLP_ROWS = 1280
BF16 = jnp.bfloat16
F32 = jnp.float32

_NT = (((1,), (1,)), ((), ()))
_TN = (((0,), (0,)), ((), ()))


def _dot(a, b):
    return jnp.dot(a, b, preferred_element_type=F32)


def _dot_nt(a, b):
    return lax.dot_general(a, b, _NT, preferred_element_type=F32)


def _dot_tn(a, b):
    return lax.dot_general(a, b, _TN, preferred_element_type=F32)


def _bf(x):
    return x.astype(BF16)


def _split_bf16(x, parts):
    out = []
    r = x
    for _ in range(parts):
        p = r.astype(BF16)
        out.append(p)
        r = r - p.astype(F32)
    return out


def _dot_exact_left(m_bf16, x, parts=3):
    acc = None
    for p in _split_bf16(x, parts):
        t = _dot(m_bf16, p)
        acc = t if acc is None else acc + t
    return acc


def _dot_exact_right(x, m_bf16, parts=3):
    acc = None
    for p in _split_bf16(x, parts):
        t = _dot(p, m_bf16)
        acc = t if acc is None else acc + t
    return acc


def _softplus(x):
    return jnp.maximum(x, 0.0) + jnp.log1p(jnp.exp(-jnp.abs(x)))


def _silu(x):
    return x * jax.nn.sigmoid(x)


def _rms(x, w):
    ms = jnp.mean(x * x, axis=-1, keepdims=True)
    return (x * lax.rsqrt(ms + NORM_EPS)) * w


def _row_tile(total_rows, target):
    n = total_rows // CHUNK
    best = 1
    for d in range(1, n + 1):
        if n % d == 0 and d * CHUNK <= target:
            best = d
    return best * CHUNK


def _col_tile(total_cols, target):
    n = total_cols // LANES
    best = 1
    for d in range(1, n + 1):
        if n % d == 0 and d * LANES <= target:
            best = d
    return best * LANES


def _seq_tile(seq, target):
    fits = [t for t in range(SUBLANES, min(seq, target) + 1, SUBLANES) if seq % t == 0]
    return max(fits) if fits else seq


def _row_offset(offset, terms):
    if all(t % SUBLANES == 0 for t in terms):
        return pl.multiple_of(offset, SUBLANES)
    return offset


def _rows_per_step(batch, target):
    return max(d for d in range(1, target + 1) if batch % d == 0)


def _const_spec(shape):
    nd = len(shape)
    return pl.BlockSpec(shape, lambda *_: (0,) * nd, pipeline_mode=pl.Buffered(1))


def _chunk_spec(bb, cols, col_block):
    return pl.BlockSpec((None, bb, CHUNK, cols), lambda s, n: (s, 0, n, col_block))


def _run(gen):
    for _ in gen:
        pass


def _inproj_kernel(h_ref, nw_ref, w_ref, ws_ref, o_ref, os_ref, *, col_tile):
    xn = _bf(_rms(h_ref[...], nw_ref[...]))
    os_ref[...] = _dot(xn, ws_ref[...])
    for c0 in range(0, MAIN_COLS, col_tile):
        o_ref[:, c0:c0 + col_tile] = _bf(_dot(xn, w_ref[:, c0:c0 + col_tile]))


def _inproj(h, norm_w, w_main, w_small):
    rows, d = h.shape
    tm = _row_tile(rows, INPROJ_ROWS)
    kern = functools.partial(_inproj_kernel, col_tile=_col_tile(MAIN_COLS, 1024))
    return pl.pallas_call(
        kern,
        out_shape=(jax.ShapeDtypeStruct((rows, MAIN_COLS), BF16),
                   jax.ShapeDtypeStruct((rows, SMALL_COLS), F32)),
        grid=(rows // tm,),
        in_specs=[
            pl.BlockSpec((tm, d), lambda i: (i, 0)),
            _const_spec((1, d)),
            _const_spec((d, MAIN_COLS)),
            _const_spec((d, SMALL_COLS)),
        ],
        out_specs=(pl.BlockSpec((tm, MAIN_COLS), lambda i: (i, 0)),
                   pl.BlockSpec((tm, SMALL_COLS), lambda i: (i, 0))),
        compiler_params=pltpu.CompilerParams(
            dimension_semantics=("arbitrary",), vmem_limit_bytes=VMEM_LIMIT),
        name="inproj",
    )(h, norm_w, w_main, w_small)


def _gla_stages(rows, valid, incl, tri, q_ref, k_ref, v_ref, z_ref, sm_ref, wg_ref, bg_ref, nw_ref,
                o_ref, st_ref):
    c = CHUNK
    probs = [(i, h) for i in rows for h in range(GLA_HEADS)]
    ks = lambda h: slice(h * GLA_DK, (h + 1) * GLA_DK)
    vs = lambda h: slice(h * GLA_DV, (h + 1) * GLA_DV)

    logit = {i: _dot(_bf(sm_ref[i]), wg_ref[...]) + bg_ref[...] for i in rows}
    yield
    g = {}
    for i in rows:
        log_sig = jnp.minimum(logit[i], 0.0) - jnp.log1p(jnp.exp(-jnp.abs(logit[i])))
        g[i] = jnp.where(valid, jnp.maximum(log_sig / GLA_GATE_NORM, GLA_LOG_DECAY_MIN), 0.0)
    cum = {i: _dot_exact_left(tri, g[i]) for i in rows}
    yield
    qe, ke, kd, dec = {}, {}, {}, {}
    for i in rows:
        cl = cum[i][c - 1:c, :]
        qe[i] = _bf((q_ref[i].astype(F32) * jnp.exp(cum[i])) * (GLA_DK ** -0.5))
        kf = k_ref[i].astype(F32)
        ke[i] = _bf(kf * jnp.exp(-cum[i]))
        kd[i] = _bf(kf * jnp.exp(cl - cum[i]))
        dec[i] = jnp.exp(cl)
        yield

    slot = [i * GLA_HEADS + h for i, h in probs]
    st = [st_ref[s] for s in slot]
    scores = [jnp.where(incl, _dot_nt(qe[i][:, ks(h)], ke[i][:, ks(h)]), 0.0) for i, h in probs]
    yield
    inter = [_dot_nt(qe[i][:, ks(h)], _bf(st[p])) for p, (i, h) in enumerate(probs)]
    yield
    kv = [_dot_tn(v_ref[i, :, vs(h)], kd[i][:, ks(h)]) for i, h in probs]
    yield
    o = [_dot(_bf(scores[p]), v_ref[i, :, vs(h)]) + inter[p] for p, (i, h) in enumerate(probs)]
    yield
    for p, (i, h) in enumerate(probs):
        st_ref[slot[p]] = st[p] * dec[i][:, ks(h)] + kv[p]
        y = _rms(o[p], nw_ref[...])
        o_ref[i, :, vs(h)] = _bf(y * _silu(z_ref[i, :, vs(h)].astype(F32)))
        yield


def _unit_lower_inverse_stages(a_list, ri, ci):
    eye = jnp.where(ri == ci, 1.0, 0.0)
    base = 8
    blk = lambda idx, size: lax.shift_right_logical(idx, size.bit_length() - 1)
    dmask = blk(ri, base) == blk(ci, base)
    dm = [jnp.where(dmask, a, 0.0) for a in a_list]
    d = [_bf(x) for x in dm]
    d2 = [_bf(_dot(x, x)) for x in d]
    yield
    t = [eye - x for x in dm]
    t = [x + _dot(_bf(x), y) for x, y in zip(t, d2)]
    yield
    d4 = [_bf(_dot(y, y)) for y in d2]
    yield
    t = [x + _dot(_bf(x), y) for x, y in zip(t, d4)]
    yield
    b = base
    while b < CHUNK:
        lmask = jnp.logical_and(blk(ri, 2 * b) == blk(ci, 2 * b), blk(ri, b) != blk(ci, b))
        l = [_bf(jnp.where(lmask, a, 0.0)) for a in a_list]
        tb = [_bf(x) for x in t]
        m = [_bf(_dot(x, y)) for x, y in zip(tb, l)]
        yield
        t = [x - _dot(y, z) for x, y, z in zip(t, m, tb)]
        yield
        b *= 2
    return t


def _dn_prepare_stages(rows, out, valid, tri, sj, ss, conv_silu, sm_ref, alog_ref, dtb_ref, eb_ref,
                       ea_ref, ea64_ref):
    c = CHUNK
    hs = lambda h: slice(h * DN_DK, (h + 1) * DN_DK)
    scale = DN_DK ** -0.5

    def l2n(x):
        return x * lax.rsqrt(jnp.sum(x * x, axis=-1, keepdims=True) + NORM_EPS)

    sm = {i: sm_ref[i] for i in rows}
    beta_blk = {i: jnp.where(valid, jax.nn.sigmoid(sm[i]), 0.0) for i in rows}
    la_blk = {i: jnp.where(valid, -jnp.exp(alog_ref[...]) * _softplus(sm[i] + dtb_ref[...]), 0.0)
              for i in rows}
    beta128 = {i: _dot_exact_right(beta_blk[i], eb_ref[...], parts=2) for i in rows}
    gam_blk = {i: _dot_exact_left(tri, la_blk[i]) for i in rows}
    la64 = {i: _dot_exact_right(la_blk[i], ea64_ref[...]) for i in rows}
    yield
    gam128 = {i: _dot_exact_right(gam_blk[i], ea_ref[...]) for i in rows}
    dsum = {i: _dot_exact_left(tri, jnp.where(sj > ss, la64[i], 0.0)) for i in rows}
    decay_all = {i: jnp.where(sj >= ss, jnp.exp(dsum[i]), 0.0) for i in rows}
    yield

    for key in ("probs", "kb", "k", "qs", "qd", "kd", "rhs", "dm", "dec"):
        out[key] = []
    for i in rows:
        for h in range(DN_HEADS):
            q = l2n(conv_silu(i, h * DN_DK))
            k = l2n(conv_silu(i, DN_KEY + h * DN_DK))
            yield
            v = conv_silu(i, 2 * DN_KEY + h * DN_DV)
            b_h = beta128[i][:, hs(h)]
            g_h = gam128[i][:, hs(h)]
            g_last = g_h[c - 1:c, :]
            eg = jnp.exp(g_h)
            kb = k * b_h
            out["probs"].append((i, h))
            out["kb"].append(_bf(kb))
            out["k"].append(_bf(k))
            out["qs"].append(_bf(q * scale))
            out["qd"].append(_bf((q * scale) * eg))
            out["kd"].append(_bf(k * jnp.exp(g_last - g_h)))
            out["rhs"].append(_bf(jnp.concatenate([v * b_h, kb * eg], axis=1)))
            out["dm"].append(decay_all[i][:, h * c:(h + 1) * c])
            out["dec"].append(jnp.exp(g_last))
            yield


def _dn_solve_stages(a, ri, ci, strict, z_ref, nw_ref, o_ref, st_ref):
    hs = lambda h: slice(h * DN_DK, (h + 1) * DN_DK)
    probs = a["probs"]
    c = CHUNK
    kq = [_dot_nt(jnp.concatenate([x, y], axis=0), z) for x, y, z in zip(a["kb"], a["qs"], a["k"])]
    yield
    a_kk = [jnp.where(strict, x[:c] * z, 0.0) for x, z in zip(kq, a["dm"])]
    a_qk = [_bf(x[c:] * z) for x, z in zip(kq, a["dm"])]
    t = yield from _unit_lower_inverse_stages(a_kk, ri, ci)
    uw = [_dot(_bf(x), y) for x, y in zip(t, a["rhs"])]
    yield
    slot = [i * DN_HEADS + h for i, h in probs]
    st = [st_ref[s] for s in slot]
    st_bf = [_bf(x) for x in st]
    ws = [_dot(jnp.concatenate([_bf(x[:, DN_DV:]), y], axis=0), z)
          for x, y, z in zip(uw, a["qd"], st_bf)]
    yield
    v_new = [_bf(x[:, :DN_DV] - y[:c]) for x, y in zip(uw, ws)]
    o = [y[c:] + _dot(z, w) for y, z, w in zip(ws, a_qk, v_new)]
    yield
    upd = [_dot_tn(x, y) for x, y in zip(a["kd"], v_new)]
    yield
    for p, (i, h) in enumerate(probs):
        st_ref[slot[p]] = st[p] * a["dec"][p] + upd[p]
        y = _rms(o[p], nw_ref[...])
        o_ref[i, :, hs(h)] = _bf(y * _silu(z_ref[i, :, hs(h)].astype(F32)))
        yield


def _gla_kernel(q_ref, k_ref, v_ref, z_ref, sm_ref, wg_ref, bg_ref, nw_ref, o_ref, st_ref, *, pad):
    bb = q_ref.shape[0]
    n = pl.program_id(1)
    c = CHUNK

    @pl.when(n == 0)
    def _():
        st_ref[...] = jnp.zeros_like(st_ref)

    row = lax.broadcasted_iota(jnp.int32, (c, 1), 0)
    valid = jnp.logical_or(n > 0, row >= pad)
    incl = (lax.broadcasted_iota(jnp.int32, (c, c), 0) >= lax.broadcasted_iota(jnp.int32, (c, c), 1))
    tri = _bf(jnp.where(incl, 1.0, 0.0))
    _run(_gla_stages(range(bb), valid, incl, tri, q_ref, k_ref, v_ref, z_ref, sm_ref, wg_ref,
                     bg_ref, nw_ref, o_ref, st_ref))


def _gla(proj, small, wg, bg, nw, bb, pad):
    steps, _, lp, _ = proj.shape
    kern = functools.partial(_gla_kernel, pad=pad)
    return pl.pallas_call(
        kern,
        out_shape=jax.ShapeDtypeStruct((steps, bb, lp, GLA_VAL), BF16),
        grid=(steps, lp // CHUNK),
        in_specs=[
            _chunk_spec(bb, GLA_KEY, COL_GQ // GLA_KEY),
            _chunk_spec(bb, GLA_KEY, COL_GK // GLA_KEY),
            _chunk_spec(bb, GLA_VAL, COL_GV // GLA_VAL),
            _chunk_spec(bb, GLA_VAL, COL_GZ // GLA_VAL),
            _chunk_spec(bb, SMALL_COLS, 0),
            _const_spec((SMALL_COLS, GLA_KEY)),
            _const_spec((1, GLA_KEY)),
            _const_spec((1, GLA_DV)),
        ],
        out_specs=_chunk_spec(bb, GLA_VAL, 0),
        scratch_shapes=[pltpu.VMEM((bb * GLA_HEADS, GLA_DV, GLA_DK), F32)],
        compiler_params=pltpu.CompilerParams(
            dimension_semantics=("arbitrary", "arbitrary"), vmem_limit_bytes=VMEM_LIMIT),
        name="gla",
    )(proj, proj, proj, proj, small, wg, bg, nw)


def _dn_kernel(qkv_ref, z_ref, sm_ref, cw_ref, alog_ref, dtb_ref, nw_ref, eb_ref, ea_ref, ea64_ref,
               o_ref, xbuf, st_ref, *, pad):
    bb = qkv_ref.shape[0]
    n = pl.program_id(1)
    c = CHUNK
    tail = SUBLANES

    @pl.when(n == 0)
    def _():
        xbuf[:, 0:tail, :] = jnp.zeros((bb, tail, xbuf.shape[2]), F32)
        st_ref[...] = jnp.zeros_like(st_ref)

    row = lax.broadcasted_iota(jnp.int32, (c, 1), 0)
    valid = jnp.logical_or(n > 0, row >= pad)
    ri = lax.broadcasted_iota(jnp.int32, (c, c), 0)
    ci = lax.broadcasted_iota(jnp.int32, (c, c), 1)
    incl = ri >= ci
    strict = ri > ci
    tri = _bf(jnp.where(incl, 1.0, 0.0))
    sj = lax.broadcasted_iota(jnp.int32, (c, DN_HEADS * c), 0)
    ss = jnp.bitwise_and(lax.broadcasted_iota(jnp.int32, (c, DN_HEADS * c), 1), c - 1)

    for i in range(bb):
        xbuf[i, tail:tail + c, :] = qkv_ref[i].astype(F32)

    def conv_silu(i, col0):
        cs = slice(col0, col0 + LANES)
        y = cw_ref[DN_CONV - 1:DN_CONV, cs] * xbuf[i, tail:tail + c, cs]
        for j in range(DN_CONV - 1):
            off = tail - (DN_CONV - 1) + j
            y = y + cw_ref[j:j + 1, cs] * xbuf[i, off:off + c, cs]
        return _silu(y)

    prepared = {}
    _run(_dn_prepare_stages(range(bb), prepared, valid, tri, sj, ss, conv_silu, sm_ref, alog_ref,
                            dtb_ref, eb_ref, ea_ref, ea64_ref))
    _run(_dn_solve_stages(prepared, ri, ci, strict, z_ref, nw_ref, o_ref, st_ref))

    for i in range(bb):
        xbuf[i, 0:tail, :] = xbuf[i, c:c + tail, :]


def _dn(proj, small, cw, alog, dtb, nw, eb, ea, ea64, bb, pad):
    steps, _, lp, _ = proj.shape
    c = CHUNK
    kern = functools.partial(_dn_kernel, pad=pad)
    return pl.pallas_call(
        kern,
        out_shape=jax.ShapeDtypeStruct((steps, bb, lp, DN_VAL), BF16),
        grid=(steps, lp // c),
        in_specs=[
            _chunk_spec(bb, DN_QKV, COL_DQKV // DN_QKV),
            _chunk_spec(bb, DN_VAL, COL_DZ // DN_VAL),
            _chunk_spec(bb, SMALL_COLS, 0),
            _const_spec((DN_CONV, DN_QKV)),
            _const_spec((1, SMALL_COLS)),
            _const_spec((1, SMALL_COLS)),
            _const_spec((1, DN_DV)),
            _const_spec((SMALL_COLS, DN_HEADS * LANES)),
            _const_spec((SMALL_COLS, DN_HEADS * LANES)),
            _const_spec((SMALL_COLS, DN_HEADS * c)),
        ],
        out_specs=_chunk_spec(bb, DN_VAL, 0),
        scratch_shapes=[pltpu.VMEM((bb, c + SUBLANES, DN_QKV), F32),
                        pltpu.VMEM((bb * DN_HEADS, DN_DK, DN_DV), F32)],
        compiler_params=pltpu.CompilerParams(
            dimension_semantics=("arbitrary", "arbitrary"), vmem_limit_bytes=VMEM_LIMIT),
        name="deltanet",
    )(proj, proj, small, cw, alog, dtb, nw, eb, ea, ea64)


def _merge_kernel(og_ref, od_ref, gg_ref, gd_ref, h_ref, wbg_ref, wbd_ref, wo_ref, o_ref):
    pg = _dot(og_ref[...], wbg_ref[...])
    pd = _dot(od_ref[...], wbd_ref[...])
    merged = (jax.nn.sigmoid(gg_ref[...].astype(F32)) * pg
              + jax.nn.sigmoid(gd_ref[...].astype(F32)) * pd)
    o_ref[...] = h_ref[...] + _dot(_bf(merged), wo_ref[...])


def _merge(o_gla, o_dn, proj, h, wbg, wbd, wo):
    rows, d = h.shape
    tm = _row_tile(rows, MERGE_ROWS)
    row_spec = lambda cols, j: pl.BlockSpec((tm, cols), lambda i: (i, j))
    return pl.pallas_call(
        _merge_kernel,
        out_shape=jax.ShapeDtypeStruct((rows, d), F32),
        grid=(rows // tm,),
        in_specs=[
            row_spec(GLA_VAL, 0),
            row_spec(DN_VAL, 0),
            row_spec(d, COL_GATES // d),
            row_spec(d, COL_GATES // d + 1),
            row_spec(d, 0),
            _const_spec((GLA_VAL, d)),
            _const_spec((DN_VAL, d)),
            _const_spec((d, d)),
        ],
        out_specs=row_spec(d, 0),
        compiler_params=pltpu.CompilerParams(
            dimension_semantics=("arbitrary",), vmem_limit_bytes=VMEM_LIMIT),
        name="merge",
    )(o_gla, o_dn, proj, proj, h, wbg, wbd, wo)


def _mlp_kernel(h_ref, nw_ref, wu_ref, wd_ref, fw_ref, o_ref, *, ff_tile, final):
    x = h_ref[...]
    xn = _bf(_rms(x, nw_ref[...]))
    acc = x
    for f in range(0, D_FF, ff_tile):
        up = jnp.maximum(_dot(xn, wu_ref[:, f:f + ff_tile]), 0.0)
        acc = acc + _dot(_bf(up * up), wd_ref[f:f + ff_tile, :])
    o_ref[...] = _rms(acc, fw_ref[...]) if final else acc


def _mlp(h, norm_w, wu, wd, final_w, final, batch, skip):
    rows, d = h.shape
    if final:
        lp = rows // batch
        seq = lp - skip
        tm = _seq_tile(seq, MLP_ROWS)
        tiles = seq // tm
        grid = (batch, tiles)
        h_spec = pl.BlockSpec((pl.Element(tm), pl.Element(d)),
                              lambda b, j: (_row_offset(b * lp + skip + j * tm, (lp, skip, tm)), 0))
        o_spec = pl.BlockSpec((tm, d), lambda b, j: (b * tiles + j, 0))
        out_rows = batch * seq
    else:
        tm = _row_tile(rows, MLP_ROWS)
        grid = (rows // tm,)
        h_spec = pl.BlockSpec((tm, d), lambda i: (i, 0))
        o_spec = pl.BlockSpec((tm, d), lambda i: (i, 0))
        out_rows = rows
    kern = functools.partial(_mlp_kernel, ff_tile=1024, final=final)
    return pl.pallas_call(
        kern,
        out_shape=jax.ShapeDtypeStruct((out_rows, d), F32),
        grid=grid,
        in_specs=[
            h_spec,
            _const_spec((1, d)),
            _const_spec((d, D_FF)),
            _const_spec((D_FF, d)),
            _const_spec((1, d)),
        ],
        out_specs=o_spec,
        compiler_params=pltpu.CompilerParams(
            dimension_semantics=("arbitrary",) * len(grid), vmem_limit_bytes=VMEM_LIMIT),
        name="mlp",
    )(h, norm_w, wu, wd, final_w)


def _lane_expand_matrix(lane0, width):
    r = jnp.arange(SMALL_COLS)[:, None]
    col_head = jnp.arange(DN_HEADS * width)[None, :] // width
    return (r == lane0 + col_head).astype(BF16)


def _pad_lanes(v, lane0):
    return jnp.zeros((1, SMALL_COLS), F32).at[0, lane0:lane0 + v.shape[0]].set(v.astype(F32))


def kernel(x, meta_tokens, mixer_norm_w, w_in, gla_w_gate_up, gla_b_gate, gla_norm_w, dn_conv_w,
           dn_a_log, dn_dt_bias, dn_norm_w, w_branch_gla, w_branch_dn, w_out, mlp_norm_w,
           w_mlp_up, w_mlp_down, final_norm_w):
    batch, seq, d = x.shape
    depth = w_in.shape[0]
    pad = (-(N_META + seq)) % CHUNK
    lp = pad + N_META + seq
    meta = jnp.broadcast_to(meta_tokens[None].astype(x.dtype), (batch, N_META, d))
    h = jnp.concatenate([jnp.zeros((batch, pad, d), x.dtype), meta, x], axis=1).reshape(batch * lp, d)

    o_lr = 2 * GLA_KEY + GLA_VAL
    o_z = o_lr + GLA_GATE_RANK
    o_db = o_z + GLA_VAL + DN_QKV + DN_VAL
    o_gate = o_db + 2 * DN_HEADS

    eb = _lane_expand_matrix(DB_LANE0, LANES)
    ea = _lane_expand_matrix(DA_LANE0, LANES)
    ea64 = _lane_expand_matrix(DA_LANE0, CHUNK)

    bb_gla = _rows_per_step(batch, 8)
    bb_dn = _rows_per_step(batch, 4)
    by_rows = lambda a, bb: a.reshape(batch // bb, bb, lp, a.shape[-1])
    flat = lambda a: a.reshape(batch * lp, a.shape[-1])

    for l in range(depth):
        w = w_in[l]
        w_main = jnp.concatenate([w[:, :o_lr], w[:, o_z:o_db], w[:, o_gate:]], axis=1).astype(BF16)
        w_small = jnp.concatenate(
            [w[:, o_lr:o_z], w[:, o_db:o_gate],
             jnp.zeros((d, SMALL_COLS - GLA_GATE_RANK - 2 * DN_HEADS), w.dtype)], axis=1).astype(BF16)
        wg = jnp.zeros((SMALL_COLS, GLA_KEY), F32).at[LR_LANE0:LR_LANE0 + GLA_GATE_RANK].set(
            gla_w_gate_up[l]).astype(BF16)

        proj, small = _inproj(h, mixer_norm_w[l][None, :], w_main, w_small)
        o_gla = _gla(by_rows(proj, bb_gla), by_rows(small, bb_gla), wg, gla_b_gate[l][None, :],
                     gla_norm_w[l][None, :], bb_gla, pad)
        o_dn = _dn(by_rows(proj, bb_dn), by_rows(small, bb_dn), dn_conv_w[l],
                   _pad_lanes(dn_a_log[l], DA_LANE0), _pad_lanes(dn_dt_bias[l], DA_LANE0),
                   dn_norm_w[l][None, :], eb, ea, ea64, bb_dn, pad)
        h = _merge(flat(o_gla), flat(o_dn), proj, h, w_branch_gla[l].astype(BF16),
                   w_branch_dn[l].astype(BF16), w_out[l].astype(BF16))
        h = _mlp(h, mlp_norm_w[l][None, :], w_mlp_up[l].astype(BF16), w_mlp_down[l].astype(BF16),
                 final_norm_w[None, :], l == depth - 1, batch, pad + N_META)

    return h.reshape(batch, seq, d)
```

```python
import functools

import jax
import jax.numpy as jnp
from jax import lax
from jax.experimental import pallas as pl
from jax.experimental.pallas import tpu as pltpu

D_MODEL = 1024
N_META = 16
CHUNK = 64
NORM_EPS = 1e-6
GLA_HEADS = 4
GLA_DK = 128
GLA_DV = 256
GLA_KEY = GLA_HEADS * GLA_DK
GLA_VAL = GLA_HEADS * GLA_DV
GLA_GATE_RANK = 16
GLA_GATE_NORM = 16.0
GLA_LOG_DECAY_MIN = -1.0
DN_HEADS = 8
DN_DK = 128
DN_DV = 128
DN_KEY = DN_HEADS * DN_DK
DN_VAL = DN_HEADS * DN_DV
DN_QKV = 2 * DN_KEY + DN_VAL
DN_CONV = 4
D_FF = 4 * D_MODEL

COL_GQ = 0
COL_GK = COL_GQ + GLA_KEY
COL_GV = COL_GK + GLA_KEY
COL_GZ = COL_GV + GLA_VAL
COL_DQKV = COL_GZ + GLA_VAL
COL_DZ = COL_DQKV + DN_QKV
COL_GATES = COL_DZ + DN_VAL
MAIN_COLS = COL_GATES + 2 * D_MODEL
LANES = 128
SUBLANES = 8
SMALL_COLS = LANES
LR_LANE0 = 0
DB_LANE0 = GLA_GATE_RANK
DA_LANE0 = GLA_GATE_RANK + DN_HEADS

VMEM_LIMIT = 56 * 1024 * 1024
INPROJ_ROWS = 640
MERGE_ROWS = 1280
MLP_ROWS = 1280
BF16 = jnp.bfloat16
F32 = jnp.float32

_NT = (((1,), (1,)), ((), ()))
_TN = (((0,), (0,)), ((), ()))


def _dot(a, b):
    return jnp.dot(a, b, preferred_element_type=F32)


def _dot_nt(a, b):
    return lax.dot_general(a, b, _NT, preferred_element_type=F32)


def _dot_tn(a, b):
    return lax.dot_general(a, b, _TN, preferred_element_type=F32)


def _bf(x):
    return x.astype(BF16)


def _split_bf16(x, parts):
    out = []
    r = x
    for _ in range(parts):
        p = r.astype(BF16)
        out.append(p)
        r = r - p.astype(F32)
    return out


def _dot_exact_left(m_bf16, x, parts=3):
    acc = None
    for p in _split_bf16(x, parts):
        t = _dot(m_bf16, p)
        acc = t if acc is None else acc + t
    return acc


def _dot_exact_right(x, m_bf16, parts=3):
    acc = None
    for p in _split_bf16(x, parts):
        t = _dot(p, m_bf16)
        acc = t if acc is None else acc + t
    return acc


def _softplus(x):
    return jnp.maximum(x, 0.0) + jnp.log1p(jnp.exp(-jnp.abs(x)))


def _silu(x):
    half = 0.5 * x
    return half + half * jnp.tanh(half)


def _rms(x, w):
    ms = jnp.mean(x * x, axis=-1, keepdims=True)
    return (x * lax.rsqrt(ms + NORM_EPS)) * w


def _row_tile(total_rows, target):
    n = total_rows // CHUNK
    best = 1
    for d in range(1, n + 1):
        if n % d == 0 and d * CHUNK <= target:
            best = d
    return best * CHUNK


def _col_tile(total_cols, target):
    n = total_cols // LANES
    best = 1
    for d in range(1, n + 1):
        if n % d == 0 and d * LANES <= target:
            best = d
    return best * LANES


def _seq_tile(seq, target):
    fits = [t for t in range(SUBLANES, min(seq, target) + 1, SUBLANES) if seq % t == 0]
    return max(fits) if fits else seq


def _row_offset(offset, terms):
    if all(t % SUBLANES == 0 for t in terms):
        return pl.multiple_of(offset, SUBLANES)
    return offset


def _rows_per_step(batch, target):
    return max(d for d in range(1, target + 1) if batch % d == 0)


def _const_spec(shape):
    nd = len(shape)
    return pl.BlockSpec(shape, lambda *_: (0,) * nd, pipeline_mode=pl.Buffered(1))


def _chunk_spec(bb, cols, col_block):
    return pl.BlockSpec((None, bb, CHUNK, cols), lambda s, n: (s, 0, n, col_block))


def _run(gen):
    for _ in gen:
        pass


def _inproj_kernel(h_ref, nw_ref, w_ref, ws_ref, o_ref, os_ref, *, col_tile):
    xn = _bf(_rms(h_ref[...], nw_ref[...]))
    os_ref[...] = _dot(xn, ws_ref[...])
    for c0 in range(0, MAIN_COLS, col_tile):
        o_ref[:, c0:c0 + col_tile] = _bf(_dot(xn, w_ref[:, c0:c0 + col_tile]))


def _inproj(h, norm_w, w_main, w_small):
    rows, d = h.shape
    tm = _row_tile(rows, INPROJ_ROWS)
    kern = functools.partial(_inproj_kernel, col_tile=_col_tile(MAIN_COLS, 1024))
    return pl.pallas_call(
        kern,
        out_shape=(jax.ShapeDtypeStruct((rows, MAIN_COLS), BF16),
                   jax.ShapeDtypeStruct((rows, SMALL_COLS), F32)),
        grid=(rows // tm,),
        in_specs=[
            pl.BlockSpec((tm, d), lambda i: (i, 0)),
            _const_spec((1, d)),
            _const_spec((d, MAIN_COLS)),
            _const_spec((d, SMALL_COLS)),
        ],
        out_specs=(pl.BlockSpec((tm, MAIN_COLS), lambda i: (i, 0)),
                   pl.BlockSpec((tm, SMALL_COLS), lambda i: (i, 0))),
        compiler_params=pltpu.CompilerParams(
            dimension_semantics=("arbitrary",), vmem_limit_bytes=VMEM_LIMIT),
        name="inproj",
    )(h, norm_w, w_main, w_small)


def _gla_stages(rows, valid, incl, tri, q_ref, k_ref, v_ref, z_ref, sm_ref, wg_ref, bg_ref, nw_ref,
                o_ref, st_ref):
    c = CHUNK
    probs = [(i, h) for i in rows for h in range(GLA_HEADS)]
    ks = lambda h: slice(h * GLA_DK, (h + 1) * GLA_DK)
    vs = lambda h: slice(h * GLA_DV, (h + 1) * GLA_DV)

    logit = {i: _dot(_bf(sm_ref[i]), wg_ref[...]) + bg_ref[...] for i in rows}
    yield
    g = {}
    for i in rows:
        log_sig = jnp.minimum(logit[i], 0.0) - jnp.log(1.0 + jnp.exp(-jnp.abs(logit[i])))
        g[i] = jnp.where(valid, jnp.maximum(log_sig / GLA_GATE_NORM, GLA_LOG_DECAY_MIN), 0.0)
    cum = {i: _dot_exact_left(tri, g[i]) for i in rows}
    yield
    qe, ke, kd, dec = {}, {}, {}, {}
    for i in rows:
        cl = cum[i][c - 1:c, :]
        qe[i] = _bf((q_ref[i].astype(F32) * jnp.exp(cum[i])) * (GLA_DK ** -0.5))
        kf = k_ref[i].astype(F32)
        ke[i] = _bf(kf * jnp.exp(-cum[i]))
        kd[i] = _bf(kf * jnp.exp(cl - cum[i]))
        dec[i] = jnp.exp(cl)
        yield

    slot = [i * GLA_HEADS + h for i, h in probs]
    st = [st_ref[s] for s in slot]
    scores = [jnp.where(incl, _dot_nt(qe[i][:, ks(h)], ke[i][:, ks(h)]), 0.0) for i, h in probs]
    yield
    inter = [_dot_nt(qe[i][:, ks(h)], _bf(st[p])) for p, (i, h) in enumerate(probs)]
    yield
    kv = [_dot_tn(v_ref[i, :, vs(h)], kd[i][:, ks(h)]) for i, h in probs]
    yield
    o = [_dot(_bf(scores[p]), v_ref[i, :, vs(h)]) + inter[p] for p, (i, h) in enumerate(probs)]
    yield
    for p, (i, h) in enumerate(probs):
        st_ref[slot[p]] = st[p] * dec[i][:, ks(h)] + kv[p]
        y = _rms(o[p], nw_ref[...])
        o_ref[i, :, vs(h)] = _bf(y * _silu(z_ref[i, :, vs(h)].astype(F32)))
        yield


def _pair_blockdiag(x, upper):
    zero = jnp.zeros_like(x)
    return jnp.concatenate([jnp.where(upper, zero, x), jnp.where(upper, x, zero)], axis=0)


def _unit_lower_inverse_stages(a_list, ri, ci, upper):
    bd = lambda x: _pair_blockdiag(x, upper)
    eye = jnp.where(ri == ci, 1.0, 0.0)
    base = 8
    blk = lambda idx, size: lax.shift_right_logical(idx, size.bit_length() - 1)
    dmask = blk(ri, base) == blk(ci, base)
    dm = [jnp.where(dmask, a, 0.0) for a in a_list]
    d = [_bf(x) for x in dm]
    d2 = [_bf(_dot(x, bd(x))) for x in d]
    yield
    t = [eye - x for x in dm]
    d2_bd = [bd(y) for y in d2]
    t = [x + _dot(_bf(x), y) for x, y in zip(t, d2_bd)]
    yield
    d4 = [_bf(_dot(x, y)) for x, y in zip(d2, d2_bd)]
    yield
    t = [x + _dot(_bf(x), bd(y)) for x, y in zip(t, d4)]
    yield
    b = base
    while b < CHUNK:
        lmask = jnp.logical_and(blk(ri, 2 * b) == blk(ci, 2 * b), blk(ri, b) != blk(ci, b))
        l = [_bf(jnp.where(lmask, a, 0.0)) for a in a_list]
        tb = [_bf(x) for x in t]
        m = [_bf(_dot(x, bd(y))) for x, y in zip(tb, l)]
        yield
        t = [x - _dot(y, bd(z)) for x, y, z in zip(t, m, tb)]
        yield
        b *= 2
    return t


def _dn_prepare_stages(rows, out, valid, tri, sj, ss, conv_silu, sm_ref, alog_ref, dtb_ref, eb_ref,
                       ea_ref, ea64_ref):
    c = CHUNK
    hs = lambda h: slice(h * DN_DK, (h + 1) * DN_DK)
    scale = DN_DK ** -0.5

    def l2n(x):
        return x * lax.rsqrt(jnp.sum(x * x, axis=-1, keepdims=True) + NORM_EPS)

    sm = {i: sm_ref[i] for i in rows}
    beta_blk = {i: jnp.where(valid, jax.nn.sigmoid(sm[i]), 0.0) for i in rows}
    la_blk = {i: jnp.where(valid, -jnp.exp(alog_ref[...]) * _softplus(sm[i] + dtb_ref[...]), 0.0)
              for i in rows}
    beta128 = {i: _dot_exact_right(beta_blk[i], eb_ref[...], parts=2) for i in rows}
    gam_blk = {i: _dot_exact_left(tri, la_blk[i]) for i in rows}
    la64 = {i: _dot_exact_right(la_blk[i], ea64_ref[...]) for i in rows}
    yield
    gam128 = {i: _dot_exact_right(gam_blk[i], ea_ref[...]) for i in rows}
    dsum = {i: _dot_exact_left(tri, jnp.where(sj > ss, la64[i], 0.0)) for i in rows}
    decay_all = {i: jnp.where(sj >= ss, jnp.exp(dsum[i]), 0.0) for i in rows}
    yield

    for key in ("probs", "qd", "kd", "rhs", "dec", "kbq", "k", "dm"):
        out[key] = []
    for i in rows:
        for h0 in range(0, DN_HEADS, 2):
            kb_pair, qs_pair, k_pair = [], [], []
            for h in (h0, h0 + 1):
                q = l2n(conv_silu(i, h * DN_DK))
                k = l2n(conv_silu(i, DN_KEY + h * DN_DK))
                yield
                v = conv_silu(i, 2 * DN_KEY + h * DN_DV)
                b_h = beta128[i][:, hs(h)]
                g_h = gam128[i][:, hs(h)]
                g_last = g_h[c - 1:c, :]
                eg = jnp.exp(g_h)
                kb = k * b_h
                kb_pair.append(_bf(kb))
                qs_pair.append(_bf(q * scale))
                k_pair.append(_bf(k))
                out["probs"].append((i, h))
                out["qd"].append(_bf((q * scale) * eg))
                out["kd"].append(_bf(k * jnp.exp(g_last - g_h)))
                out["rhs"].append(_bf(jnp.concatenate([v * b_h, kb * eg], axis=1)))
                out["dec"].append(jnp.exp(g_last))
                yield
            out["kbq"].append(jnp.concatenate([jnp.concatenate(kb_pair, axis=1),
                                               jnp.concatenate(qs_pair, axis=1)], axis=0))
            out["k"].append(jnp.concatenate(k_pair, axis=1))
            out["dm"].append(decay_all[i][:, h0 * c:(h0 + 2) * c])


def _dn_solve_stages(a, ri, ci, upper, z_ref, nw_ref, o_ref, st_ref):
    hs = lambda h: slice(h * DN_DK, (h + 1) * DN_DK)
    probs = a["probs"]
    c = CHUNK
    lanes_q = lax.broadcasted_iota(jnp.int32, (c, 2 * DN_DK), 1) >= DN_DK
    kq = [_dot_nt(x, _pair_blockdiag(y, lanes_q)) for x, y in zip(a["kbq"], a["k"])]
    yield
    a_kk = [jnp.where(ri > ci, x[:c] * z, 0.0) for x, z in zip(kq, a["dm"])]
    a_qk = [_pair_blockdiag(_bf(x[c:] * z), upper) for x, z in zip(kq, a["dm"])]
    t = yield from _unit_lower_inverse_stages(a_kk, ri, ci, upper)
    rhs = a["rhs"]
    uw_pair = [_dot(_pair_blockdiag(_bf(x), upper), jnp.concatenate([rhs[2 * j], rhs[2 * j + 1]], axis=0))
               for j, x in enumerate(t)]
    uw = [x[r0:r0 + c] for x in uw_pair for r0 in (0, c)]
    yield
    slot = [i * DN_HEADS + h for i, h in probs]
    st = [st_ref[s] for s in slot]
    st_bf = [_bf(x) for x in st]
    ws = [_dot(jnp.concatenate([_bf(x[:, DN_DV:]), y], axis=0), z)
          for x, y, z in zip(uw, a["qd"], st_bf)]
    yield
    v_new = [_bf(x[:, :DN_DV] - y[:c]) for x, y in zip(uw, ws)]
    intra_pair = [_dot(x, jnp.concatenate([v_new[2 * j], v_new[2 * j + 1]], axis=0))
                  for j, x in enumerate(a_qk)]
    intra = [x[r0:r0 + c] for x in intra_pair for r0 in (0, c)]
    o = [y[c:] + z for y, z in zip(ws, intra)]
    yield
    upd = [_dot_tn(x, y) for x, y in zip(a["kd"], v_new)]
    yield
    for p, (i, h) in enumerate(probs):
        st_ref[slot[p]] = st[p] * a["dec"][p] + upd[p]
        y = _rms(o[p], nw_ref[...])
        o_ref[i, :, hs(h)] = _bf(y * _silu(z_ref[i, :, hs(h)].astype(F32)))
        yield


def _gla_kernel(q_ref, k_ref, v_ref, z_ref, sm_ref, wg_ref, bg_ref, nw_ref, o_ref, st_ref, *, pad):
    bb = q_ref.shape[0]
    n = pl.program_id(1)
    c = CHUNK

    @pl.when(n == 0)
    def _():
        st_ref[...] = jnp.zeros_like(st_ref)

    row = lax.broadcasted_iota(jnp.int32, (c, 1), 0)
    valid = jnp.logical_or(n > 0, row >= pad)
    incl = (lax.broadcasted_iota(jnp.int32, (c, c), 0) >= lax.broadcasted_iota(jnp.int32, (c, c), 1))
    tri = _bf(jnp.where(incl, 1.0, 0.0))
    _run(_gla_stages(range(bb), valid, incl, tri, q_ref, k_ref, v_ref, z_ref, sm_ref, wg_ref,
                     bg_ref, nw_ref, o_ref, st_ref))


def _gla(proj, small, wg, bg, nw, bb, pad):
    steps, _, lp, _ = proj.shape
    kern = functools.partial(_gla_kernel, pad=pad)
    return pl.pallas_call(
        kern,
        out_shape=jax.ShapeDtypeStruct((steps, bb, lp, GLA_VAL), BF16),
        grid=(steps, lp // CHUNK),
        in_specs=[
            _chunk_spec(bb, GLA_KEY, COL_GQ // GLA_KEY),
            _chunk_spec(bb, GLA_KEY, COL_GK // GLA_KEY),
            _chunk_spec(bb, GLA_VAL, COL_GV // GLA_VAL),
            _chunk_spec(bb, GLA_VAL, COL_GZ // GLA_VAL),
            _chunk_spec(bb, SMALL_COLS, 0),
            _const_spec((SMALL_COLS, GLA_KEY)),
            _const_spec((1, GLA_KEY)),
            _const_spec((1, GLA_DV)),
        ],
        out_specs=_chunk_spec(bb, GLA_VAL, 0),
        scratch_shapes=[pltpu.VMEM((bb * GLA_HEADS, GLA_DV, GLA_DK), F32)],
        compiler_params=pltpu.CompilerParams(
            dimension_semantics=("arbitrary", "arbitrary"), vmem_limit_bytes=VMEM_LIMIT),
        name="gla",
    )(proj, proj, proj, proj, small, wg, bg, nw)


def _dn_kernel(qkv_ref, z_ref, sm_ref, cw_ref, alog_ref, dtb_ref, nw_ref, eb_ref, ea_ref, ea64_ref,
               o_ref, xbuf, st_ref, *, pad):
    bb = qkv_ref.shape[0]
    n = pl.program_id(1)
    c = CHUNK
    tail = SUBLANES

    @pl.when(n == 0)
    def _():
        xbuf[:, 0:tail, :] = jnp.zeros((bb, tail, xbuf.shape[2]), F32)
        st_ref[...] = jnp.zeros_like(st_ref)

    row = lax.broadcasted_iota(jnp.int32, (c, 1), 0)
    valid = jnp.logical_or(n > 0, row >= pad)
    incl = (lax.broadcasted_iota(jnp.int32, (c, c), 0) >= lax.broadcasted_iota(jnp.int32, (c, c), 1))
    tri = _bf(jnp.where(incl, 1.0, 0.0))
    ri = lax.broadcasted_iota(jnp.int32, (c, 2 * c), 0)
    lane = lax.broadcasted_iota(jnp.int32, (c, 2 * c), 1)
    ci = jnp.bitwise_and(lane, c - 1)
    upper = lane >= c
    sj = lax.broadcasted_iota(jnp.int32, (c, DN_HEADS * c), 0)
    ss = jnp.bitwise_and(lax.broadcasted_iota(jnp.int32, (c, DN_HEADS * c), 1), c - 1)

    for i in range(bb):
        xbuf[i, tail:tail + c, :] = qkv_ref[i].astype(F32)

    def conv_silu(i, col0):
        cs = slice(col0, col0 + LANES)
        y = cw_ref[DN_CONV - 1:DN_CONV, cs] * xbuf[i, tail:tail + c, cs]
        for j in range(DN_CONV - 1):
            off = tail - (DN_CONV - 1) + j
            y = y + cw_ref[j:j + 1, cs] * xbuf[i, off:off + c, cs]
        return _silu(y)

    prepared = {}
    _run(_dn_prepare_stages(range(bb), prepared, valid, tri, sj, ss, conv_silu, sm_ref, alog_ref,
                            dtb_ref, eb_ref, ea_ref, ea64_ref))
    _run(_dn_solve_stages(prepared, ri, ci, upper, z_ref, nw_ref, o_ref, st_ref))

    for i in range(bb):
        xbuf[i, 0:tail, :] = xbuf[i, c:c + tail, :]


def _dn(proj, small, cw, alog, dtb, nw, eb, ea, ea64, bb, pad):
    steps, _, lp, _ = proj.shape
    c = CHUNK
    kern = functools.partial(_dn_kernel, pad=pad)
    return pl.pallas_call(
        kern,
        out_shape=jax.ShapeDtypeStruct((steps, bb, lp, DN_VAL), BF16),
        grid=(steps, lp // c),
        in_specs=[
            _chunk_spec(bb, DN_QKV, COL_DQKV // DN_QKV),
            _chunk_spec(bb, DN_VAL, COL_DZ // DN_VAL),
            _chunk_spec(bb, SMALL_COLS, 0),
            _const_spec((DN_CONV, DN_QKV)),
            _const_spec((1, SMALL_COLS)),
            _const_spec((1, SMALL_COLS)),
            _const_spec((1, DN_DV)),
            _const_spec((SMALL_COLS, DN_HEADS * LANES)),
            _const_spec((SMALL_COLS, DN_HEADS * LANES)),
            _const_spec((SMALL_COLS, DN_HEADS * c)),
        ],
        out_specs=_chunk_spec(bb, DN_VAL, 0),
        scratch_shapes=[pltpu.VMEM((bb, c + SUBLANES, DN_QKV), F32),
                        pltpu.VMEM((bb * DN_HEADS, DN_DK, DN_DV), F32)],
        compiler_params=pltpu.CompilerParams(
            dimension_semantics=("arbitrary", "arbitrary"), vmem_limit_bytes=VMEM_LIMIT),
        name="deltanet",
    )(proj, proj, small, cw, alog, dtb, nw, eb, ea, ea64)


def _merge_kernel(og_ref, od_ref, gg_ref, gd_ref, h_ref, wbg_ref, wbd_ref, wo_ref, o_ref):
    pg = _dot(og_ref[...], wbg_ref[...])
    pd = _dot(od_ref[...], wbd_ref[...])
    merged = (jax.nn.sigmoid(gg_ref[...].astype(F32)) * pg
              + jax.nn.sigmoid(gd_ref[...].astype(F32)) * pd)
    o_ref[...] = h_ref[...] + _dot(_bf(merged), wo_ref[...])


def _merge(o_gla, o_dn, proj, h, wbg, wbd, wo):
    rows, d = h.shape
    tm = _row_tile(rows, MERGE_ROWS)
    row_spec = lambda cols, j: pl.BlockSpec((tm, cols), lambda i: (i, j))
    return pl.pallas_call(
        _merge_kernel,
        out_shape=jax.ShapeDtypeStruct((rows, d), F32),
        grid=(rows // tm,),
        in_specs=[
            row_spec(GLA_VAL, 0),
            row_spec(DN_VAL, 0),
            row_spec(d, COL_GATES // d),
            row_spec(d, COL_GATES // d + 1),
            row_spec(d, 0),
            _const_spec((GLA_VAL, d)),
            _const_spec((DN_VAL, d)),
            _const_spec((d, d)),
        ],
        out_specs=row_spec(d, 0),
        compiler_params=pltpu.CompilerParams(
            dimension_semantics=("arbitrary",), vmem_limit_bytes=VMEM_LIMIT),
        name="merge",
    )(o_gla, o_dn, proj, proj, h, wbg, wbd, wo)


def _mlp_kernel(h_ref, nw_ref, wu_ref, wd_ref, fw_ref, o_ref, *, ff_tile, final):
    x = h_ref[...]
    xn = _bf(_rms(x, nw_ref[...]))
    acc = x
    for f in range(0, D_FF, ff_tile):
        up = jnp.maximum(_dot(xn, wu_ref[:, f:f + ff_tile]), 0.0)
        acc = acc + _dot(_bf(up * up), wd_ref[f:f + ff_tile, :])
    o_ref[...] = _rms(acc, fw_ref[...]) if final else acc


def _mlp(h, norm_w, wu, wd, final_w, final, batch, skip):
    rows, d = h.shape
    if final:
        lp = rows // batch
        seq = lp - skip
        tm = _seq_tile(seq, MLP_ROWS)
        tiles = seq // tm
        grid = (batch, tiles)
        h_spec = pl.BlockSpec((pl.Element(tm), pl.Element(d)),
                              lambda b, j: (_row_offset(b * lp + skip + j * tm, (lp, skip, tm)), 0))
        o_spec = pl.BlockSpec((tm, d), lambda b, j: (b * tiles + j, 0))
        out_rows = batch * seq
    else:
        tm = _row_tile(rows, MLP_ROWS)
        grid = (rows // tm,)
        h_spec = pl.BlockSpec((tm, d), lambda i: (i, 0))
        o_spec = pl.BlockSpec((tm, d), lambda i: (i, 0))
        out_rows = rows
    kern = functools.partial(_mlp_kernel, ff_tile=1024, final=final)
    return pl.pallas_call(
        kern,
        out_shape=jax.ShapeDtypeStruct((out_rows, d), F32),
        grid=grid,
        in_specs=[
            h_spec,
            _const_spec((1, d)),
            _const_spec((d, D_FF)),
            _const_spec((D_FF, d)),
            _const_spec((1, d)),
        ],
        out_specs=o_spec,
        compiler_params=pltpu.CompilerParams(
            dimension_semantics=("arbitrary",) * len(grid), vmem_limit_bytes=VMEM_LIMIT),
        name="mlp",
    )(h, norm_w, wu, wd, final_w)


def _lane_expand_matrix(lane0, width):
    r = jnp.arange(SMALL_COLS)[:, None]
    col_head = jnp.arange(DN_HEADS * width)[None, :] // width
    return (r == lane0 + col_head).astype(BF16)


def _pad_lanes(v, lane0):
    return jnp.zeros((1, SMALL_COLS), F32).at[0, lane0:lane0 + v.shape[0]].set(v.astype(F32))


def kernel(x, meta_tokens, mixer_norm_w, w_in, gla_w_gate_up, gla_b_gate, gla_norm_w, dn_conv_w,
           dn_a_log, dn_dt_bias, dn_norm_w, w_branch_gla, w_branch_dn, w_out, mlp_norm_w,
           w_mlp_up, w_mlp_down, final_norm_w):
    batch, seq, d = x.shape
    depth = w_in.shape[0]
    pad = (-(N_META + seq)) % CHUNK
    lp = pad + N_META + seq
    meta = jnp.broadcast_to(meta_tokens[None].astype(x.dtype), (batch, N_META, d))
    h = jnp.concatenate([jnp.zeros((batch, pad, d), x.dtype), meta, x], axis=1).reshape(batch * lp, d)

    o_lr = 2 * GLA_KEY + GLA_VAL
    o_z = o_lr + GLA_GATE_RANK
    o_db = o_z + GLA_VAL + DN_QKV + DN_VAL
    o_gate = o_db + 2 * DN_HEADS

    eb = _lane_expand_matrix(DB_LANE0, LANES)
    ea = _lane_expand_matrix(DA_LANE0, LANES)
    ea64 = _lane_expand_matrix(DA_LANE0, CHUNK)

    bb_gla = _rows_per_step(batch, 8)
    bb_dn = _rows_per_step(batch, 4)
    by_rows = lambda a, bb: a.reshape(batch // bb, bb, lp, a.shape[-1])
    flat = lambda a: a.reshape(batch * lp, a.shape[-1])

    for l in range(depth):
        w = w_in[l]
        w_main = jnp.concatenate(
            [w[:, :o_lr].astype(BF16), w[:, o_z:o_db].astype(BF16), w[:, o_gate:].astype(BF16)], axis=1)
        w_small = jnp.concatenate(
            [w[:, o_lr:o_z].astype(BF16), w[:, o_db:o_gate].astype(BF16),
             jnp.zeros((d, SMALL_COLS - GLA_GATE_RANK - 2 * DN_HEADS), BF16)], axis=1)
        wg = jnp.zeros((SMALL_COLS, GLA_KEY), F32).at[LR_LANE0:LR_LANE0 + GLA_GATE_RANK].set(
            gla_w_gate_up[l]).astype(BF16)

        proj, small = _inproj(h, mixer_norm_w[l][None, :], w_main, w_small)
        o_gla = _gla(by_rows(proj, bb_gla), by_rows(small, bb_gla), wg, gla_b_gate[l][None, :],
                     gla_norm_w[l][None, :], bb_gla, pad)
        o_dn = _dn(by_rows(proj, bb_dn), by_rows(small, bb_dn), dn_conv_w[l],
                   _pad_lanes(dn_a_log[l], DA_LANE0), _pad_lanes(dn_dt_bias[l], DA_LANE0),
                   dn_norm_w[l][None, :], eb, ea, ea64, bb_dn, pad)
        h = _merge(flat(o_gla), flat(o_dn), proj, h, w_branch_gla[l].astype(BF16),
                   w_branch_dn[l].astype(BF16), w_out[l].astype(BF16))
        h = _mlp(h, mlp_norm_w[l][None, :], w_mlp_up[l].astype(BF16), w_mlp_down[l].astype(BF16),
                 final_norm_w[None, :], l == depth - 1, batch, pad + N_META)

    return h.reshape(batch, seq, d)
```
